```python
import math
import jax, jax.numpy as jnp
from jax import lax
import numpy as np

D_MODEL = 1024
BATCH = 8
SEQ = 4096
DEPTH = 4

CHUNK = 64
Q_BLOCK = 128
N_EVEN = (DEPTH + 1) // 2
N_ODD = DEPTH // 2
EPS = 1e-6

RET_HEADS = 4
RET_DK = 64
RET_DV = 128
ROPE_BASE = 10000.0

SSD_HEADS = 8
SSD_HEAD_DIM = 64
SSD_INNER = SSD_HEADS * SSD_HEAD_DIM
SSD_GROUPS = 2
SSD_STATE = 64
SSD_CONV = 4
SSD_CONV_DIM = SSD_INNER + 2 * SSD_GROUPS * SSD_STATE
SSD_NORM_GROUP = SSD_INNER // SSD_GROUPS

HG_HEADS = 4
HG_DK = 128
HG_DV = 128

FOX_HEADS = 4
FOX_DIM = 128

FFN_HIDDEN = -(-8 * D_MODEL // (3 * 256)) * 256

AB_SIZES = (RET_HEADS * RET_DK, RET_HEADS * RET_DK, RET_HEADS * RET_DV, RET_HEADS * RET_DV,
            SSD_INNER, SSD_CONV_DIM, SSD_HEADS)
AB_IN = sum(AB_SIZES)
AB_OUT = RET_HEADS * RET_DV + SSD_INNER
CD_SIZES = (HG_HEADS * HG_DK, HG_HEADS * HG_DK, HG_HEADS * HG_DV, HG_HEADS * HG_DV,
            FOX_HEADS * FOX_DIM, FOX_HEADS * FOX_DIM, FOX_HEADS * FOX_DIM, FOX_HEADS)
CD_IN = sum(CD_SIZES)
CD_OUT = HG_HEADS * HG_DV + FOX_HEADS * FOX_DIM

kernel_name = "hybrid_retention_ssd_hgrn2_fox_trunk"


def rmsnorm(x, w):
    xf = x.astype(jnp.float32)
    y = xf * lax.rsqrt(jnp.mean(xf * xf, axis=-1, keepdims=True) + EPS)
    return (y * w.astype(jnp.float32)).astype(x.dtype)


def split_cols(h, sizes):
    offs, acc = [], 0
    for s in sizes[:-1]:
        acc += s
        offs.append(acc)
    return jnp.split(h, offs, axis=-1)


def to_heads(t, n):
    b, t_len, _ = t.shape
    return t.reshape(b, t_len, n, -1).transpose(0, 2, 1, 3)


def merge_heads(t):
    b, n, t_len, d = t.shape
    return t.transpose(0, 2, 1, 3).reshape(b, t_len, n * d)


def rotary_every_two(x):
    t_len, d = x.shape[2], x.shape[3]
    half = d // 2
    freqs = ROPE_BASE ** (-jnp.linspace(0.0, 1.0, half, dtype=jnp.float32))
    ang = jnp.arange(t_len, dtype=jnp.float32)[:, None] * freqs[None, :]
    cos, sin = jnp.cos(ang), jnp.sin(ang)
    xf = x.astype(jnp.float32).reshape(x.shape[:-1] + (half, 2))
    x1, x2 = xf[..., 0], xf[..., 1]
    out = jnp.stack([x1 * cos - x2 * sin, x1 * sin + x2 * cos], axis=-1)
    return out.reshape(x.shape).astype(x.dtype)


def chunk_recurrence(q, k, v, log_a):
    b_, h_, t_len, dk = q.shape
    dv = v.shape[-1]
    n_chunks = t_len // CHUNK
    per_channel = log_a.ndim == 4

    def to_chunks(a):
        return jnp.moveaxis(a.reshape(a.shape[:2] + (n_chunks, CHUNK) + a.shape[3:]), 2, 0)

    causal = jnp.tril(jnp.ones((CHUNK, CHUNK), dtype=bool))

    def step(state, inp):
        qi, ki, vi, li = inp
        cum = jnp.cumsum(li.astype(jnp.float32), axis=2)
        if per_channel:
            diff = cum[:, :, :, None, :] - cum[:, :, None, :, :]
            decay = jnp.exp(jnp.where(causal[:, :, None], diff, -jnp.inf))
            scores = jnp.einsum('bhid,bhjd,bhijd->bhij', qi, ki, decay)
            q_in = qi * jnp.exp(cum)
            k_out = ki * jnp.exp(cum[:, :, -1:] - cum)
            a_last = jnp.exp(cum[:, :, -1])[..., None]
        else:
            diff = cum[:, :, :, None] - cum[:, :, None, :]
            decay = jnp.exp(jnp.where(causal, diff, -jnp.inf))
            scores = jnp.einsum('bhid,bhjd->bhij', qi, ki) * decay
            q_in = qi * jnp.exp(cum)[..., None]
            k_out = ki * jnp.exp(cum[:, :, -1:] - cum)[..., None]
            a_last = jnp.exp(cum[:, :, -1])[..., None, None]
        out = (jnp.einsum('bhij,bhjv->bhiv', scores, vi)
               + jnp.einsum('bhid,bhdv->bhiv', q_in, state))
        state = a_last * state + jnp.einsum('bhjd,bhjv->bhdv', k_out, vi)
        return state, out

    state0 = jnp.zeros((b_, h_, dk, dv), jnp.float32)
    _, out = lax.scan(step, state0, (to_chunks(q), to_chunks(k), to_chunks(v), to_chunks(log_a)))
    return jnp.moveaxis(out, 0, 2).reshape(b_, h_, t_len, dv).astype(v.dtype)


def retention_mixer(rq, rk, rv, rg, gn_w):
    q = rotary_every_two(to_heads(rq, RET_HEADS))
    k = rotary_every_two(to_heads(rk, RET_HEADS)) * (RET_DK ** -0.5)
    v = to_heads(rv, RET_HEADS)
    log_gamma = jnp.log1p(-jnp.exp2(-5.0 - jnp.arange(RET_HEADS, dtype=jnp.float32)))
    log_a = jnp.broadcast_to(log_gamma[None, :, None], q.shape[:3])
    o = chunk_recurrence(q, k, v, log_a).astype(jnp.float32)
    mu = jnp.mean(o, axis=-1, keepdims=True)
    var = jnp.mean(jnp.square(o - mu), axis=-1, keepdims=True)
    o = (o - mu) * lax.rsqrt(var + EPS) * gn_w[:, None, :].astype(jnp.float32)
    return merge_heads(o).astype(rg.dtype) * jax.nn.silu(rg)


def ssd_mixer(z, xbc, dt_raw, conv_w, conv_b, dt_bias, a_log, d_skip, norm_w):
    xbc = lax.conv_general_dilated(xbc, conv_w[:, None, :], window_strides=(1,),
                                   padding=[(SSD_CONV - 1, 0)],
                                   dimension_numbers=('NWC', 'WIO', 'NWC'),
                                   feature_group_count=SSD_CONV_DIM)
    xbc = jax.nn.silu(xbc + conv_b)
    xs, bm, cm = split_cols(xbc, (SSD_INNER, SSD_GROUPS * SSD_STATE, SSD_GROUPS * SSD_STATE))
    v = to_heads(xs, SSD_HEADS)
    rep = SSD_HEADS // SSD_GROUPS
    bm = jnp.repeat(to_heads(bm, SSD_GROUPS), rep, axis=1)
    cm = jnp.repeat(to_heads(cm, SSD_GROUPS), rep, axis=1)
    dt = jax.nn.softplus(dt_raw.astype(jnp.float32) + dt_bias.astype(jnp.float32))
    dt = dt.transpose(0, 2, 1)
    log_a = dt * (-jnp.exp(a_log.astype(jnp.float32)))[None, :, None]
    y = chunk_recurrence(cm, bm * dt[..., None], v, log_a) + d_skip[None, :, None, None] * v
    y = merge_heads(y) * jax.nn.silu(z)
    b_, t_len, _ = y.shape
    yg = y.reshape(b_, t_len, SSD_GROUPS, SSD_NORM_GROUP)
    yg = rmsnorm(yg, jnp.ones((SSD_NORM_GROUP,), y.dtype)).reshape(b_, t_len, SSD_INNER)
    return yg * norm_w


def hgrn2_mixer(hq, hf, hi, hg, lower_bound, norm_w):
    q = to_heads(hq, HG_HEADS)
    zf = to_heads(hf, HG_HEADS).astype(jnp.float32)
    lb = lower_bound.reshape(HG_HEADS, 1, HG_DK)
    f = lb + (1.0 - lb) * jax.nn.sigmoid(zf)
    k = (1.0 - lb) * jax.nn.sigmoid(-zf)
    o = chunk_recurrence(q, k, to_heads(hi, HG_HEADS), jnp.log(f))
    o = rmsnorm(o, norm_w)
    return merge_heads(o) * jax.nn.silu(hg)


def fox_mixer(fq, fk, fv, f_raw, f_bias, qn_w, kn_w):
    q = rmsnorm(to_heads(fq, FOX_HEADS), qn_w)
    k = rmsnorm(to_heads(fk, FOX_HEADS), kn_w)
    v = to_heads(fv, FOX_HEADS)
    log_f = jax.nn.log_sigmoid(f_raw.astype(jnp.float32) + f_bias.astype(jnp.float32))
    cum_f = jnp.cumsum(log_f.transpose(0, 2, 1), axis=-1)
    scale = FOX_DIM ** -0.5
    t_len = q.shape[2]
    outs = []
    for start in range(0, t_len, Q_BLOCK):
        end = start + Q_BLOCK
        s = jnp.einsum('bhqd,bhkd->bhqk', q[:, :, start:end], k[:, :, :end]).astype(jnp.float32)
        s = s * scale + cum_f[:, :, start:end, None] - cum_f[:, :, None, :end]
        mask = (start + jnp.arange(Q_BLOCK))[:, None] >= jnp.arange(end)[None, :]
        p = jax.nn.softmax(jnp.where(mask, s, -jnp.inf), axis=-1)
        outs.append(jnp.einsum('bhqk,bhkd->bhqd', p.astype(v.dtype), v[:, :, :end]))
    return merge_heads(jnp.concatenate(outs, axis=2))


def swiglu(u, w_gate, w_up, w_down):
    return (jax.nn.silu(u @ w_gate) * (u @ w_up)) @ w_down


def setup_inputs(seed: int = 0) -> dict:
    key = jax.random.key(seed)
    ks = jax.random.split(key, 24)
    f32 = jnp.float32

    def nrm(k, shape, scale):
        return jax.random.normal(k, shape, f32) * scale

    def gain(k, shape):
        return 1.0 + 0.1 * jax.random.normal(k, shape, f32)

    dt0 = jnp.exp(jax.random.uniform(ks[12], (N_EVEN, SSD_HEADS), f32,
                                     math.log(1e-3), math.log(1e-1)))
    return {
        "x": nrm(ks[0], (BATCH, SEQ, D_MODEL), 1.0),
        "norm_mix": gain(ks[1], (DEPTH, D_MODEL)),
        "norm_ffn": gain(ks[2], (DEPTH, D_MODEL)),
        "ffn_w_gate": nrm(ks[3], (DEPTH, D_MODEL, FFN_HIDDEN), D_MODEL ** -0.5),
        "ffn_w_up": nrm(ks[4], (DEPTH, D_MODEL, FFN_HIDDEN), D_MODEL ** -0.5),
        "ffn_w_down": nrm(ks[5], (DEPTH, FFN_HIDDEN, D_MODEL), FFN_HIDDEN ** -0.5),
        "ab_w_in": nrm(ks[6], (N_EVEN, D_MODEL, AB_IN), D_MODEL ** -0.5),
        "ab_w_out": nrm(ks[7], (N_EVEN, AB_OUT, D_MODEL), AB_OUT ** -0.5),
        "ret_gn_w": gain(ks[8], (N_EVEN, RET_HEADS, RET_DV)),
        "ssd_conv_w": nrm(ks[9], (N_EVEN, SSD_CONV, SSD_CONV_DIM), SSD_CONV ** -0.5),
        "ssd_conv_b": nrm(ks[10], (N_EVEN, SSD_CONV_DIM), 0.02),
        "ssd_dt_bias": dt0 + jnp.log(-jnp.expm1(-dt0)),
        "ssd_a_log": jnp.log(jax.random.uniform(ks[11], (N_EVEN, SSD_HEADS), f32, 1.0, 16.0)),
        "ssd_d": gain(ks[13], (N_EVEN, SSD_HEADS)),
        "ssd_norm_w": gain(ks[14], (N_EVEN, SSD_INNER)),
        "cd_w_in": nrm(ks[15], (N_ODD, D_MODEL, CD_IN), D_MODEL ** -0.5),
        "cd_w_out": nrm(ks[16], (N_ODD, CD_OUT, D_MODEL), CD_OUT ** -0.5),
        "hg_lb_logits": nrm(ks[17], (N_ODD, HG_HEADS * HG_DK), 0.1),
        "hg_norm_w": gain(ks[18], (N_ODD, HG_DV)),
        "fox_f_bias": jax.random.uniform(ks[19], (N_ODD, FOX_HEADS), f32, 1.0, 5.0),
        "fox_q_norm_w": gain(ks[20], (N_ODD, FOX_DIM)),
        "fox_k_norm_w": gain(ks[21], (N_ODD, FOX_DIM)),
    }


def reference(x, norm_mix, norm_ffn, ffn_w_gate, ffn_w_up, ffn_w_down,
              ab_w_in, ab_w_out, ret_gn_w, ssd_conv_w, ssd_conv_b, ssd_dt_bias,
              ssd_a_log, ssd_d, ssd_norm_w, cd_w_in, cd_w_out, hg_lb_logits,
              hg_norm_w, fox_f_bias, fox_q_norm_w, fox_k_norm_w):
    lb_cum = jnp.cumsum(jax.nn.softmax(hg_lb_logits.astype(jnp.float32), axis=0), axis=0)
    lower_bounds = lb_cum - lb_cum[:1]

    h = x
    for layer in range(DEPTH):
        j = layer // 2
        u = rmsnorm(h, norm_mix[layer])
        if layer % 2 == 0:
            rq, rk, rv, rg, z, xbc, dt_raw = split_cols(u @ ab_w_in[j], AB_SIZES)
            y_ret = retention_mixer(rq, rk, rv, rg, ret_gn_w[j])
            y_ssd = ssd_mixer(z, xbc, dt_raw, ssd_conv_w[j], ssd_conv_b[j], ssd_dt_bias[j],
                              ssd_a_log[j], ssd_d[j], ssd_norm_w[j])
            h = h + jnp.concatenate([y_ret, y_ssd], axis=-1) @ ab_w_out[j]
        else:
            hq, hf, hi, hg, fq, fk, fv, f_raw = split_cols(u @ cd_w_in[j], CD_SIZES)
            y_hg = hgrn2_mixer(hq, hf, hi, hg, lower_bounds[j], hg_norm_w[j])
            y_fox = fox_mixer(fq, fk, fv, f_raw, fox_f_bias[j], fox_q_norm_w[j], fox_k_norm_w[j])
            h = h + jnp.concatenate([y_hg, y_fox], axis=-1) @ cd_w_out[j]
        u = rmsnorm(h, norm_ffn[layer])
        h = h + swiglu(u, ffn_w_gate[layer], ffn_w_up[layer], ffn_w_down[layer])
    return h
```

```python
import functools
import math

import numpy as np
import jax
import jax.numpy as jnp
from jax import lax
from jax.experimental import pallas as pl
from jax.experimental.pallas import tpu as pltpu

F32 = jnp.float32
BF16 = jnp.bfloat16

D_MODEL = 1024
EPS = 1e-6
ROPE_BASE = 10000.0

RET_HEADS, RET_DK, RET_DV = 4, 64, 128
SSD_HEADS, SSD_HEAD_DIM, SSD_GROUPS, SSD_STATE, SSD_CONV = 8, 64, 2, 64, 4
SSD_INNER = SSD_HEADS * SSD_HEAD_DIM
SSD_CONV_DIM = SSD_INNER + 2 * SSD_GROUPS * SSD_STATE
HG_HEADS, HG_DIM = 4, 128
FOX_HEADS, FOX_DIM = 4, 128
FFN_HIDDEN = 2816

AB_MAIN = 2816
AB_GATES = SSD_HEADS
CD_MAIN = 3584
CD_GATES = FOX_HEADS
GATE_PAD = 128

V7X_LANES = 128
V7X_VMEM_LIMIT = 56 * 1024 * 1024

ROW_TILE = 512
COL_TILE = 512
FFN_TILE = 256
CHUNK = 128
SUB = 16
ATT_BLOCK = 256


def _dot(a, b):
    return jnp.dot(a, b, preferred_element_type=F32)


def _dot_nt(a, b):
    return lax.dot_general(a, b, (((1,), (1,)), ((), ())), preferred_element_type=F32)


def _dot_tn(a, b):
    return lax.dot_general(a, b, (((0,), (0,)), ((), ())), preferred_element_type=F32)


def _split3(x):
    hi = x.astype(BF16)
    r = x - hi.astype(F32)
    mid = r.astype(BF16)
    lo = (r - mid.astype(F32)).astype(BF16)
    return hi, mid, lo


def _tri_dot(tri, x):
    hi, mid, lo = _split3(x)
    return _dot(tri, hi) + _dot(tri, mid) + _dot(tri, lo)


def _dot_tri(x, tri):
    hi, mid, lo = _split3(x)
    return _dot(hi, tri) + _dot(mid, tri) + _dot(lo, tri)


def _sigmoid(x):
    return 1.0 / (1.0 + jnp.exp(-x))


def _softplus(x):
    return jnp.maximum(x, 0.0) + jnp.log1p(jnp.exp(-jnp.abs(x)))


def _rms(x, w):
    return x * lax.rsqrt(jnp.mean(x * x, axis=-1, keepdims=True) + EPS) * w


def _const_spec(shape):
    zeros = (0,) * len(shape)
    return pl.BlockSpec(shape, lambda *_: zeros, pipeline_mode=pl.Buffered(1))


def _params(semantics):
    return pltpu.CompilerParams(dimension_semantics=semantics,
                                vmem_limit_bytes=V7X_VMEM_LIMIT)


def _inproj_kernel(h_ref, nw_ref, w_ref, wg_ref, qkw_ref, out_ref, gate_ref, *, norm_cols):
    x = h_ref[...]
    ub = _rms(x, nw_ref[...]).astype(BF16)
    n_main = w_ref.shape[1]
    for c0 in range(0, n_main, COL_TILE):
        width = min(COL_TILE, n_main - c0)
        y = _dot(ub, w_ref[:, c0:c0 + width])
        for l0 in range(0, width, V7X_LANES):
            col = c0 + l0
            piece = y[:, l0:l0 + V7X_LANES]
            if norm_cols[0] <= col < norm_cols[1]:
                piece = _rms(piece, qkw_ref[:, col - norm_cols[0]:col - norm_cols[0] + V7X_LANES])
            out_ref[:, col:col + V7X_LANES] = piece.astype(out_ref.dtype)
    gate_ref[...] = _dot(ub, wg_ref[...])


def _inproj(h2d, norm_w, w_main, w_gate, qk_w, norm_cols):
    n_rows = h2d.shape[0]
    n_main = w_main.shape[1]
    kern = functools.partial(_inproj_kernel, norm_cols=norm_cols)
    return pl.pallas_call(
        kern,
        grid=(n_rows // ROW_TILE,),
        in_specs=[
            pl.BlockSpec((ROW_TILE, D_MODEL), lambda i: (i, 0)),
            _const_spec((1, D_MODEL)),
            _const_spec((D_MODEL, n_main)),
            _const_spec((D_MODEL, GATE_PAD)),
            _const_spec(qk_w.shape),
        ],
        out_specs=[
            pl.BlockSpec((ROW_TILE, n_main), lambda i: (i, 0)),
            pl.BlockSpec((ROW_TILE, GATE_PAD), lambda i: (i, 0)),
        ],
        out_shape=[
            jax.ShapeDtypeStruct((n_rows, n_main), BF16),
            jax.ShapeDtypeStruct((n_rows, GATE_PAD), F32),
        ],
        compiler_params=_params(("parallel",)),
        name="inproj",
    )(h2d, norm_w, w_main, w_gate, qk_w)


def _ffn_kernel(ya_ref, yb_ref, h_ref, woa_ref, wob_ref, nw_ref, wg_ref, wu_ref, wd_ref,
                out_ref, act_ref):
    h1 = h_ref[...] + _dot(ya_ref[...], woa_ref[...]) + _dot(yb_ref[...], wob_ref[...])
    ub = _rms(h1, nw_ref[...]).astype(BF16)
    for c0 in range(0, FFN_HIDDEN, FFN_TILE):
        g = _dot(ub, wg_ref[:, c0:c0 + FFN_TILE])
        up = _dot(ub, wu_ref[:, c0:c0 + FFN_TILE])
        act_ref[:, c0:c0 + FFN_TILE] = (g * _sigmoid(g) * up).astype(BF16)
    out_ref[...] = h1 + _dot(act_ref[...], wd_ref[...])


def _outproj_ffn(ya, yb, h2d, wo_a, wo_b, norm_w, w_gate, w_up, w_down):
    n_rows = h2d.shape[0]
    half = ya.shape[1]
    return pl.pallas_call(
        _ffn_kernel,
        grid=(n_rows // ROW_TILE,),
        in_specs=[
            pl.BlockSpec((ROW_TILE, half), lambda i: (i, 0)),
            pl.BlockSpec((ROW_TILE, half), lambda i: (i, 0)),
            pl.BlockSpec((ROW_TILE, D_MODEL), lambda i: (i, 0)),
            _const_spec((half, D_MODEL)),
            _const_spec((half, D_MODEL)),
            _const_spec((1, D_MODEL)),
            _const_spec((D_MODEL, FFN_HIDDEN)),
            _const_spec((D_MODEL, FFN_HIDDEN)),
            _const_spec((FFN_HIDDEN, D_MODEL)),
        ],
        out_specs=pl.BlockSpec((ROW_TILE, D_MODEL), lambda i: (i, 0)),
        out_shape=jax.ShapeDtypeStruct((n_rows, D_MODEL), F32),
        scratch_shapes=[pltpu.VMEM((ROW_TILE, FFN_HIDDEN), BF16)],
        compiler_params=_params(("parallel",)),
        name="outproj_ffn",
    )(ya, yb, h2d, wo_a, wo_b, norm_w, w_gate, w_up, w_down)


def _even_mixer_kernel(proj_ref, gate_ref, gate_t_ref, cos_ref, sin_ref, gnw_ref,
                       convw_ref, convb_ref, dtb_ref, dtb_t_ref, alog_ref, alog_t_ref,
                       dskip_ref, normw_ref,
                       yret_ref, yssd_ref,
                       ret_state, ssd_state, conv_buf):
    c = CHUNK
    t = pl.program_id(1)

    @pl.when(t == 0)
    def _():
        ret_state[...] = jnp.zeros_like(ret_state)
        ssd_state[...] = jnp.zeros_like(ssd_state)
        conv_buf[0:8, :] = jnp.zeros((8, SSD_CONV_DIM), F32)

    row_i = lax.broadcasted_iota(jnp.int32, (c, c), 0)
    col_j = lax.broadcasted_iota(jnp.int32, (c, c), 1)
    causal = row_i >= col_j
    lane = lax.broadcasted_iota(jnp.int32, (1, V7X_LANES), 1)
    pos = lax.broadcasted_iota(jnp.int32, (c, 1), 0).astype(F32)

    cos = cos_ref[...]
    sin = sin_ref[...]
    q1 = proj_ref[0, :, 0:128].astype(F32)
    q2 = proj_ref[0, :, 128:256].astype(F32)
    k1 = proj_ref[0, :, 256:384].astype(F32)
    k2 = proj_ref[0, :, 384:512].astype(F32)
    rq1 = q1 * cos - q2 * sin
    rq2 = q1 * sin + q2 * cos
    k_scale = RET_DK ** -0.5
    rk1 = (k1 * cos - k2 * sin) * k_scale
    rk2 = (k1 * sin + k2 * cos) * k_scale
    k_all = jnp.concatenate([rk1, rk2], axis=-1)
    k_all_b = k_all.astype(BF16)
    dij = (row_i - col_j).astype(F32)
    lane_head = lane // (RET_DK // 2)
    for h in range(RET_HEADS):
        log_g = math.log1p(-(2.0 ** (-5 - h)))
        hm = lane_head == h
        qm = jnp.concatenate([jnp.where(hm, rq1, 0.0), jnp.where(hm, rq2, 0.0)],
                             axis=-1).astype(BF16)
        decay = jnp.exp(jnp.where(causal, dij * log_g, -jnp.inf))
        p = (_dot_nt(qm, k_all_b) * decay).astype(BF16)
        v_h = proj_ref[0, :, 512 + 128 * h:640 + 128 * h]
        s_h = ret_state[:, 128 * h:128 * h + 128].astype(BF16)
        o = _dot(p, v_h) + _dot(qm, s_h) * jnp.exp((pos + 1.0) * log_g)
        mu = jnp.mean(o, axis=-1, keepdims=True)
        d = o - mu
        var = jnp.mean(d * d, axis=-1, keepdims=True)
        on = d * lax.rsqrt(var + EPS) * gnw_ref[:, 128 * h:128 * h + 128]
        g = proj_ref[0, :, 1024 + 128 * h:1152 + 128 * h].astype(F32)
        yret_ref[0, :, 128 * h:128 * h + 128] = (on * (g * _sigmoid(g))).astype(yret_ref.dtype)
    lane2 = lax.broadcasted_iota(jnp.int32, (1, 2 * V7X_LANES), 1)
    lane2_head = (lane2 % V7X_LANES) // (RET_DK // 2)
    row2 = lax.broadcasted_iota(jnp.int32, (2 * V7X_LANES, 1), 0)
    row2_head = (row2 % V7X_LANES) // (RET_DK // 2)
    log_g_lane = jnp.zeros((1, 2 * V7X_LANES), F32)
    log_g_row = jnp.zeros((2 * V7X_LANES, 1), F32)
    for h in range(RET_HEADS):
        log_g = math.log1p(-(2.0 ** (-5 - h)))
        log_g_lane = jnp.where(lane2_head == h, log_g, log_g_lane)
        log_g_row = jnp.where(row2_head == h, log_g, log_g_row)
    k_out = (k_all * jnp.exp((float(c - 1) - pos) * log_g_lane)).astype(BF16)
    v_all = proj_ref[0, :, 512:1024]
    ret_state[...] = ret_state[...] * jnp.exp(float(c) * log_g_row) + _dot_tn(k_out, v_all)

    conv_buf[8:8 + c, :] = proj_ref[0, :, 2048:2816].astype(F32)
    acc = jnp.zeros((c, SSD_CONV_DIM), F32) + convb_ref[...]
    for k in range(SSD_CONV):
        off = 8 - (SSD_CONV - 1) + k
        acc = acc + convw_ref[k:k + 1, :] * conv_buf[off:off + c, :]
    conv_buf[0:8, :] = conv_buf[c:c + 8, :]
    xbc = acc * _sigmoid(acc)
    xs = xbc[:, 0:SSD_INNER]
    bm = xbc[:, 512:640]
    cm = xbc[:, 640:768]

    dt = _softplus(gate_ref[0] + dtb_ref[...])
    la = -dt * jnp.exp(alog_ref[...])
    dt_t = _softplus(gate_t_ref[0] + dtb_t_ref[...])
    la_t = -dt_t * jnp.exp(alog_t_ref[...])
    tri = jnp.where(causal, 1.0, 0.0).astype(BF16)
    tri_t = jnp.where(row_i <= col_j, 1.0, 0.0).astype(BF16)
    cum = _tri_dot(tri, la)
    cum_t = _dot_tri(la_t, tri_t)
    cum_last = cum[c - 1:c, :]
    lane_half = lane // SSD_HEAD_DIM
    y_pairs = []
    for g in range(SSD_GROUPS):
        gm = lane_half == g
        cg = jnp.where(gm, cm, 0.0)
        bg = jnp.where(gm, bm, 0.0)
        cg_b = cg.astype(BF16)
        gmat = _dot_nt(cg_b, bg.astype(BF16))
        heads_per_group = SSD_HEADS // SSD_GROUPS
        for hh in range(heads_per_group):
            h = g * heads_per_group + hh
            pair, half = h // 2, h % 2
            x_pair = xs[:, 128 * pair:128 * pair + 128]
            xm = jnp.where(lane_half == half, x_pair, 0.0).astype(BF16)
            cc = cum[:, h:h + 1]
            cr = cum_t[h:h + 1, :]
            lmat = jnp.exp(jnp.where(causal, cc - cr, -jnp.inf)) * dt_t[h:h + 1, :]
            p = (gmat * lmat).astype(BF16)
            s_h = ssd_state[h]
            y = _dot(p, xm) + _dot((cg * jnp.exp(cc)).astype(BF16), s_h.astype(BF16))
            cl = cum_last[:, h:h + 1]
            k_out = (bg * (dt[:, h:h + 1] * jnp.exp(cl - cc))).astype(BF16)
            ssd_state[h] = s_h * jnp.exp(cl) + _dot_tn(k_out, xm)
            if half == 0:
                y_pairs.append(y)
            else:
                y_pairs[pair] = y_pairs[pair] + y
    y = jnp.concatenate(y_pairs, axis=-1) + dskip_ref[...] * xs
    z = proj_ref[0, :, 1536:2048].astype(F32)
    y = y * (z * _sigmoid(z))
    group = SSD_INNER // SSD_GROUPS
    for g in range(SSD_GROUPS):
        yg = y[:, group * g:group * (g + 1)]
        ones = jnp.ones((1, group), F32)
        yssd_ref[0, :, group * g:group * (g + 1)] = (
            _rms(yg, ones) * normw_ref[:, group * g:group * (g + 1)]).astype(yssd_ref.dtype)


def _even_mixer(proj, gates, gates_t, cos, sin, gn_w, conv_w, conv_b, dt_b, dt_b_t,
                a_log, a_log_t, d_skip, norm_w):
    b, t_len, _ = proj.shape
    c = CHUNK
    half = RET_HEADS * RET_DV
    return pl.pallas_call(
        _even_mixer_kernel,
        grid=(b, t_len // c),
        in_specs=[
            pl.BlockSpec((1, c, AB_MAIN), lambda i, j: (i, j, 0)),
            pl.BlockSpec((1, c, GATE_PAD), lambda i, j: (i, j, 0)),
            pl.BlockSpec((1, 8, c), lambda i, j: (i, 0, j)),
            pl.BlockSpec((c, V7X_LANES), lambda i, j: (j, 0)),
            pl.BlockSpec((c, V7X_LANES), lambda i, j: (j, 0)),
            _const_spec(gn_w.shape), _const_spec(conv_w.shape), _const_spec(conv_b.shape),
            _const_spec(dt_b.shape), _const_spec(dt_b_t.shape),
            _const_spec(a_log.shape), _const_spec(a_log_t.shape),
            _const_spec(d_skip.shape), _const_spec(norm_w.shape),
        ],
        out_specs=[
            pl.BlockSpec((1, c, half), lambda i, j: (i, j, 0)),
            pl.BlockSpec((1, c, SSD_INNER), lambda i, j: (i, j, 0)),
        ],
        out_shape=[
            jax.ShapeDtypeStruct((b, t_len, half), BF16),
            jax.ShapeDtypeStruct((b, t_len, SSD_INNER), BF16),
        ],
        scratch_shapes=[
            pltpu.VMEM((2 * V7X_LANES, RET_HEADS * RET_DV), F32),
            pltpu.VMEM((SSD_HEADS, V7X_LANES, V7X_LANES), F32),
            pltpu.VMEM((c + 8, SSD_CONV_DIM), F32),
        ],
        compiler_params=_params(("parallel", "arbitrary")),
        name="even_mixer",
    )(proj, gates, gates_t, cos, sin, gn_w, conv_w, conv_b, dt_b, dt_b_t,
      a_log, a_log_t, d_skip, norm_w)


def _hgrn2_kernel(proj_ref, lbl_ref, normw_ref, y_ref, state_t, o_buf, *, layer):
    c = CHUNK
    t = pl.program_id(1)

    @pl.when(t == 0)
    def _():
        state_t[...] = jnp.zeros_like(state_t)

    logits = lbl_ref[...]
    e = jnp.exp(logits - jnp.max(logits, axis=0, keepdims=True))
    prob = e / jnp.sum(e, axis=0, keepdims=True)
    lb = jnp.zeros((1, HG_HEADS * HG_DIM), F32)
    for i in range(1, layer + 1):
        lb = lb + prob[i:i + 1, :]

    row_i = lax.broadcasted_iota(jnp.int32, (c, c), 0)
    col_j = lax.broadcasted_iota(jnp.int32, (c, c), 1)
    tri = jnp.where(row_i >= col_j, 1.0, 0.0).astype(BF16)

    width = HG_HEADS * HG_DIM
    zf = proj_ref[0, :, width:2 * width].astype(F32)
    f = lb + (1.0 - lb) * _sigmoid(zf)
    k_all = (1.0 - lb) * _sigmoid(-zf)
    cum_all = _tri_dot(tri, jnp.log(f))

    sub_i = lax.broadcasted_iota(jnp.int32, (SUB, SUB), 0)
    sub_j = lax.broadcasted_iota(jnp.int32, (SUB, SUB), 1)
    for h in range(HG_HEADS):
        sl = slice(HG_DIM * h, HG_DIM * (h + 1))
        q = proj_ref[0, :, HG_DIM * h:HG_DIM * (h + 1)].astype(F32)
        k = k_all[:, sl]
        cum = cum_all[:, sl]
        v = proj_ref[0, :, 2 * width + HG_DIM * h:2 * width + HG_DIM * (h + 1)]
        st = state_t[h]
        o_buf[...] = _dot_nt((q * jnp.exp(cum)).astype(BF16), st.astype(BF16))
        for blk in range(c // SUB):
            lo, hi = blk * SUB, (blk + 1) * SUB
            if blk == 0:
                ref_row = jnp.zeros((1, HG_DIM), F32)
            else:
                ref_row = cum[lo - 1:lo, :]
            qb = (q[lo:hi, :] * jnp.exp(cum[lo:hi, :] - ref_row)).astype(BF16)
            kb = (k[0:hi, :] * jnp.exp(ref_row - cum[0:hi, :])).astype(BF16)
            s = _dot_nt(qb, kb)
            if blk == 0:
                s = jnp.where(sub_i >= sub_j, s, 0.0)
            else:
                s = jnp.concatenate(
                    [s[:, 0:lo], jnp.where(sub_i >= sub_j, s[:, lo:hi], 0.0)], axis=-1)
            o_buf[lo:hi, :] += _dot(s.astype(BF16), v[0:hi, :])
        cl = cum[c - 1:c, :]
        k_out = (k * jnp.exp(cl - cum)).astype(BF16)
        state_t[h] = st * jnp.exp(cl) + _dot_tn(v, k_out)
        o = o_buf[...]
        g = proj_ref[0, :, 3 * width + HG_DIM * h:3 * width + HG_DIM * (h + 1)].astype(F32)
        y_ref[0, :, sl] = (_rms(o, normw_ref[...]) * (g * _sigmoid(g))).astype(y_ref.dtype)


def _hgrn2(proj, lb_logits, norm_w, layer):
    b, t_len, _ = proj.shape
    c = CHUNK
    width = HG_HEADS * HG_DIM
    kern = functools.partial(_hgrn2_kernel, layer=layer)
    return pl.pallas_call(
        kern,
        grid=(b, t_len // c),
        in_specs=[
            pl.BlockSpec((1, c, 4 * width), lambda i, j: (i, j, 0)),
            _const_spec(lb_logits.shape),
            _const_spec(norm_w.shape),
        ],
        out_specs=pl.BlockSpec((1, c, width), lambda i, j: (i, j, 0)),
        out_shape=jax.ShapeDtypeStruct((b, t_len, width), BF16),
        scratch_shapes=[
            pltpu.VMEM((HG_HEADS, HG_DIM, HG_DIM), F32),
            pltpu.VMEM((c, HG_DIM), F32),
        ],
        compiler_params=_params(("parallel", "arbitrary")),
        name="hgrn2",
    )(proj, lb_logits, norm_w)


def _forget_cumsum_kernel(gate_t_ref, bias_t_ref, out_ref):
    t_len = gate_t_ref.shape[2]
    blk = V7X_LANES
    x = gate_t_ref[0] + bias_t_ref[...]
    log_f = jnp.minimum(x, 0.0) - jnp.log1p(jnp.exp(-jnp.abs(x)))
    row_i = lax.broadcasted_iota(jnp.int32, (blk, blk), 0)
    col_j = lax.broadcasted_iota(jnp.int32, (blk, blk), 1)
    tri_t = jnp.where(row_i <= col_j, 1.0, 0.0).astype(BF16)
    carry = jnp.zeros((8, 1), F32)
    for i in range(t_len // blk):
        cum = _dot_tri(log_f[:, i * blk:(i + 1) * blk], tri_t) + carry
        out_ref[0, :, i * blk:(i + 1) * blk] = cum
        carry = cum[:, blk - 1:blk]


def _forget_cumsum(gates_t, bias_t):
    b, _, t_len = gates_t.shape
    return pl.pallas_call(
        _forget_cumsum_kernel,
        grid=(b,),
        in_specs=[pl.BlockSpec((1, 8, t_len), lambda i: (i, 0, 0)), _const_spec(bias_t.shape)],
        out_specs=pl.BlockSpec((1, 8, t_len), lambda i: (i, 0, 0)),
        out_shape=jax.ShapeDtypeStruct((b, 8, t_len), F32),
        compiler_params=_params(("parallel",)),
        name="forget_cumsum",
    )(gates_t, bias_t)


def _fox_kernel(q_ref, k_ref, v_ref, f_ref, y_ref):
    blk = ATT_BLOCK
    qi = pl.program_id(2)
    q = q_ref[0]
    q0 = pl.multiple_of(qi * blk, blk)
    f_q0 = f_ref[0, :, pl.ds(q0, V7X_LANES)][:, 0:1]

    def block(kj, carry, masked):
        m, l, acc = carry
        k0 = pl.multiple_of(kj * blk, blk)
        k = k_ref[0, pl.ds(k0, blk), :]
        v = v_ref[0, pl.ds(k0, blk), :]
        s = _dot_nt(q, k) + (f_q0 - f_ref[0, :, pl.ds(k0, blk)])
        if masked:
            row_i = lax.broadcasted_iota(jnp.int32, (blk, blk), 0)
            col_j = lax.broadcasted_iota(jnp.int32, (blk, blk), 1)
            s = jnp.where(row_i >= col_j, s, -jnp.inf)
        m_new = jnp.maximum(m, jnp.max(s, axis=-1, keepdims=True))
        alpha = jnp.exp(m - m_new)
        p = jnp.exp(s - m_new)
        l = alpha * l + jnp.sum(p, axis=-1, keepdims=True)
        acc = alpha * acc + _dot(p.astype(BF16), v)
        return m_new, l, acc

    init = (jnp.full((blk, 1), -jnp.inf, F32), jnp.zeros((blk, 1), F32),
            jnp.zeros((blk, FOX_DIM), F32))
    carry = lax.fori_loop(0, qi, lambda kj, cr: block(kj, cr, False), init)
    m, l, acc = block(qi, carry, True)
    y_ref[0] = (acc / l).astype(y_ref.dtype)


def _fox(proj, f_cum):
    b, t_len, _ = proj.shape
    blk = ATT_BLOCK
    qcol, kcol, vcol = (2048 // FOX_DIM, 2560 // FOX_DIM, 3072 // FOX_DIM)
    return pl.pallas_call(
        _fox_kernel,
        grid=(b, FOX_HEADS, t_len // blk),
        in_specs=[
            pl.BlockSpec((1, blk, FOX_DIM), lambda i, h, j: (i, j, qcol + h)),
            pl.BlockSpec((1, t_len, FOX_DIM), lambda i, h, j: (i, 0, kcol + h)),
            pl.BlockSpec((1, t_len, FOX_DIM), lambda i, h, j: (i, 0, vcol + h)),
            pl.BlockSpec((1, 1, t_len), lambda i, h, j: (i * FOX_HEADS + h, 0, 0)),
        ],
        out_specs=pl.BlockSpec((1, blk, FOX_DIM), lambda i, h, j: (i, j, h)),
        out_shape=jax.ShapeDtypeStruct((b, t_len, FOX_HEADS * FOX_DIM), BF16),
        compiler_params=_params(("parallel", "parallel", "arbitrary")),
        name="fox_attention",
    )(proj, proj, proj, f_cum)


def _rotary_perm():
    perm = np.zeros((RET_HEADS * RET_DK,), np.int32)
    for half in range(2):
        for h in range(RET_HEADS):
            for i in range(RET_DK // 2):
                perm[half * 128 + h * 32 + i] = h * RET_DK + 2 * i + half
    return perm


def _pad_lanes(a, width=GATE_PAD):
    return jnp.pad(a, [(0, 0)] * (a.ndim - 1) + [(0, width - a.shape[-1])])


def kernel(x, norm_mix, norm_ffn, ffn_w_gate, ffn_w_up, ffn_w_down, ab_w_in, ab_w_out, ret_gn_w,
           ssd_conv_w, ssd_conv_b, ssd_dt_bias, ssd_a_log, ssd_d, ssd_norm_w, cd_w_in, cd_w_out,
           hg_lb_logits, hg_norm_w, fox_f_bias, fox_q_norm_w, fox_k_norm_w):
    b, t_len, d = x.shape
    depth = norm_mix.shape[0]
    n_rows = b * t_len
    perm = _rotary_perm()

    half = RET_DK // 2
    freqs = ROPE_BASE ** (-jnp.linspace(0.0, 1.0, half, dtype=F32))
    ang = jnp.arange(t_len, dtype=F32)[:, None] * freqs[None, :]
    cos = jnp.tile(jnp.cos(ang), (1, RET_HEADS))
    sin = jnp.tile(jnp.sin(ang), (1, RET_HEADS))

    h2d = x.reshape(n_rows, d)
    for layer in range(depth):
        j = layer // 2
        if layer % 2 == 0:
            w = ab_w_in[j]
            w_main = jnp.concatenate(
                [w[:, 0:256][:, perm], w[:, 256:512][:, perm], w[:, 512:AB_MAIN]], axis=1).astype(BF16)
            w_gate = _pad_lanes(w[:, AB_MAIN:AB_MAIN + AB_GATES]).astype(BF16)
            dummy = jnp.ones((1, V7X_LANES), F32)
            proj, gates = _inproj(h2d, norm_mix[layer][None, :], w_main, w_gate, dummy, (0, 0))
            gates3 = gates.reshape(b, t_len, GATE_PAD)
            gates_t = jnp.transpose(gates3[:, :, 0:8], (0, 2, 1))
            y_a, y_b = _even_mixer(
                proj.reshape(b, t_len, AB_MAIN), gates3, gates_t, cos, sin,
                ret_gn_w[j].reshape(1, RET_HEADS * RET_DV),
                ssd_conv_w[j], ssd_conv_b[j][None, :],
                _pad_lanes(ssd_dt_bias[j][None, :]), ssd_dt_bias[j][:, None],
                _pad_lanes(ssd_a_log[j][None, :]), ssd_a_log[j][:, None],
                jnp.repeat(ssd_d[j], SSD_HEAD_DIM)[None, :], ssd_norm_w[j][None, :])
            w_out = ab_w_out[j]
        else:
            w = cd_w_in[j]
            w_main = w[:, 0:CD_MAIN].astype(BF16)
            w_gate = _pad_lanes(w[:, CD_MAIN:CD_MAIN + CD_GATES]).astype(BF16)
            qk_w = jnp.concatenate([jnp.tile(fox_q_norm_w[j] * (FOX_DIM ** -0.5), FOX_HEADS),
                                    jnp.tile(fox_k_norm_w[j], FOX_HEADS)])[None, :]
            proj, gates = _inproj(h2d, norm_mix[layer][None, :], w_main, w_gate, qk_w, (2048, 3072))
            proj3 = proj.reshape(b, t_len, CD_MAIN)
            gates_t = jnp.transpose(gates.reshape(b, t_len, GATE_PAD)[:, :, 0:8], (0, 2, 1))
            bias_t = jnp.pad(fox_f_bias[j], (0, 8 - FOX_HEADS))[:, None]
            f_cum = _forget_cumsum(gates_t, bias_t)[:, 0:FOX_HEADS, :].reshape(b * FOX_HEADS, 1, t_len)
            y_a = _hgrn2(proj3, hg_lb_logits, hg_norm_w[j][None, :], j)
            y_b = _fox(proj3, f_cum)
            w_out = cd_w_out[j]
        half_out = w_out.shape[0] // 2
        h2d = _outproj_ffn(
            y_a.reshape(n_rows, half_out), y_b.reshape(n_rows, half_out), h2d,
            w_out[0:half_out].astype(BF16), w_out[half_out:].astype(BF16),
            norm_ffn[layer][None, :],
            ffn_w_gate[layer].astype(BF16), ffn_w_up[layer].astype(BF16), ffn_w_down[layer].astype(BF16))
    return h2d.reshape(b, t_len, d)
```

```python
import functools
import math

import numpy as np
import jax
import jax.numpy as jnp
from jax import lax
from jax.experimental import pallas as pl
from jax.experimental.pallas import tpu as pltpu

F32 = jnp.float32
BF16 = jnp.bfloat16

D_MODEL = 1024
EPS = 1e-6
ROPE_BASE = 10000.0

RET_HEADS, RET_DK, RET_DV = 4, 64, 128
SSD_HEADS, SSD_HEAD_DIM, SSD_GROUPS, SSD_STATE, SSD_CONV = 8, 64, 2, 64, 4
SSD_INNER = SSD_HEADS * SSD_HEAD_DIM
SSD_CONV_DIM = SSD_INNER + 2 * SSD_GROUPS * SSD_STATE
HG_HEADS, HG_DIM = 4, 128
FOX_HEADS, FOX_DIM = 4, 128
FFN_HIDDEN = 2816

AB_MAIN = 2816
AB_GATES = SSD_HEADS
CD_MAIN = 3584
CD_GATES = FOX_HEADS
GATE_PAD = 128

V7X_LANES = 128
V7X_VMEM_LIMIT = 56 * 1024 * 1024

ROW_TILE = 512
COL_TILE = 512
FFN_TILE = 256
CHUNK = 128
SUB = 16
ATT_ROWS = 1024
ATT_SUB = 256
ATT_KEYS = 512


def _dot(a, b):
    return jnp.dot(a, b, preferred_element_type=F32)


def _dot_nt(a, b):
    return lax.dot_general(a, b, (((1,), (1,)), ((), ())), preferred_element_type=F32)


def _dot_tn(a, b):
    return lax.dot_general(a, b, (((0,), (0,)), ((), ())), preferred_element_type=F32)


def _split3(x):
    hi = x.astype(BF16)
    r = x - hi.astype(F32)
    mid = r.astype(BF16)
    lo = (r - mid.astype(F32)).astype(BF16)
    return hi, mid, lo


def _tri_dot(tri, x):
    hi, mid, lo = _split3(x)
    return _dot(tri, hi) + _dot(tri, mid) + _dot(tri, lo)


def _dot_tri(x, tri):
    hi, mid, lo = _split3(x)
    return _dot(hi, tri) + _dot(mid, tri) + _dot(lo, tri)


def _sigmoid(x):
    return 1.0 / (1.0 + jnp.exp(-x))


def _softplus(x):
    return jnp.maximum(x, 0.0) + jnp.log1p(jnp.exp(-jnp.abs(x)))


def _rms(x, w):
    return x * lax.rsqrt(jnp.mean(x * x, axis=-1, keepdims=True) + EPS) * w


def _const_spec(shape):
    zeros = (0,) * len(shape)
    return pl.BlockSpec(shape, lambda *_: zeros, pipeline_mode=pl.Buffered(1))


def _params(semantics):
    return pltpu.CompilerParams(dimension_semantics=semantics,
                                vmem_limit_bytes=V7X_VMEM_LIMIT)


def _inproj_kernel(h_ref, nw_ref, w_ref, wg_ref, qkw_ref, out_ref, gate_ref, *, norm_cols):
    x = h_ref[...]
    ub = _rms(x, nw_ref[...]).astype(BF16)
    n_main = w_ref.shape[1]
    for c0 in range(0, n_main, COL_TILE):
        width = min(COL_TILE, n_main - c0)
        y = _dot(ub, w_ref[:, c0:c0 + width])
        for l0 in range(0, width, V7X_LANES):
            col = c0 + l0
            piece = y[:, l0:l0 + V7X_LANES]
            if norm_cols[0] <= col < norm_cols[1]:
                piece = _rms(piece, qkw_ref[:, col - norm_cols[0]:col - norm_cols[0] + V7X_LANES])
            out_ref[:, col:col + V7X_LANES] = piece.astype(out_ref.dtype)
    gate_ref[...] = _dot(ub, wg_ref[...])


def _inproj(h2d, norm_w, w_main, w_gate, qk_w, norm_cols):
    n_rows = h2d.shape[0]
    n_main = w_main.shape[1]
    kern = functools.partial(_inproj_kernel, norm_cols=norm_cols)
    return pl.pallas_call(
        kern,
        grid=(n_rows // ROW_TILE,),
        in_specs=[
            pl.BlockSpec((ROW_TILE, D_MODEL), lambda i: (i, 0)),
            _const_spec((1, D_MODEL)),
            _const_spec((D_MODEL, n_main)),
            _const_spec((D_MODEL, GATE_PAD)),
            _const_spec(qk_w.shape),
        ],
        out_specs=[
            pl.BlockSpec((ROW_TILE, n_main), lambda i: (i, 0)),
            pl.BlockSpec((ROW_TILE, GATE_PAD), lambda i: (i, 0)),
        ],
        out_shape=[
            jax.ShapeDtypeStruct((n_rows, n_main), BF16),
            jax.ShapeDtypeStruct((n_rows, GATE_PAD), F32),
        ],
        compiler_params=_params(("parallel",)),
        name="inproj",
    )(h2d, norm_w, w_main, w_gate, qk_w)


def _ffn_kernel(ya_ref, yb_ref, h_ref, woa_ref, wob_ref, nw_ref, wg_ref, wu_ref, wd_ref,
                out_ref, act_ref):
    h1 = h_ref[...] + _dot(ya_ref[...], woa_ref[...]) + _dot(yb_ref[...], wob_ref[...])
    ub = _rms(h1, nw_ref[...]).astype(BF16)
    for c0 in range(0, FFN_HIDDEN, FFN_TILE):
        g = _dot(ub, wg_ref[:, c0:c0 + FFN_TILE])
        up = _dot(ub, wu_ref[:, c0:c0 + FFN_TILE])
        act_ref[:, c0:c0 + FFN_TILE] = (g * _sigmoid(g) * up).astype(BF16)
    out_ref[...] = h1 + _dot(act_ref[...], wd_ref[...])


def _outproj_ffn(ya, yb, h2d, wo_a, wo_b, norm_w, w_gate, w_up, w_down):
    n_rows = h2d.shape[0]
    half = ya.shape[1]
    return pl.pallas_call(
        _ffn_kernel,
        grid=(n_rows // ROW_TILE,),
        in_specs=[
            pl.BlockSpec((ROW_TILE, half), lambda i: (i, 0)),
            pl.BlockSpec((ROW_TILE, half), lambda i: (i, 0)),
            pl.BlockSpec((ROW_TILE, D_MODEL), lambda i: (i, 0)),
            _const_spec((half, D_MODEL)),
            _const_spec((half, D_MODEL)),
            _const_spec((1, D_MODEL)),
            _const_spec((D_MODEL, FFN_HIDDEN)),
            _const_spec((D_MODEL, FFN_HIDDEN)),
            _const_spec((FFN_HIDDEN, D_MODEL)),
        ],
        out_specs=pl.BlockSpec((ROW_TILE, D_MODEL), lambda i: (i, 0)),
        out_shape=jax.ShapeDtypeStruct((n_rows, D_MODEL), F32),
        scratch_shapes=[pltpu.VMEM((ROW_TILE, FFN_HIDDEN), BF16)],
        compiler_params=_params(("parallel",)),
        name="outproj_ffn",
    )(ya, yb, h2d, wo_a, wo_b, norm_w, w_gate, w_up, w_down)


def _even_mixer_kernel(proj_ref, gate_ref, gate_t_ref, cos_ref, sin_ref, gnw_ref,
                       convw_ref, convb_ref, dtb_ref, dtb_t_ref, alog_ref, alog_t_ref,
                       dskip_ref, normw_ref,
                       yret_ref, yssd_ref,
                       ret_state, ssd_state, conv_buf):
    c = CHUNK
    t = pl.program_id(1)

    @pl.when(t == 0)
    def _():
        ret_state[...] = jnp.zeros_like(ret_state)
        ssd_state[...] = jnp.zeros_like(ssd_state)
        conv_buf[0:8, :] = jnp.zeros((8, SSD_CONV_DIM), F32)

    row_i = lax.broadcasted_iota(jnp.int32, (c, c), 0)
    col_j = lax.broadcasted_iota(jnp.int32, (c, c), 1)
    causal = row_i >= col_j
    lane = lax.broadcasted_iota(jnp.int32, (1, V7X_LANES), 1)
    pos = lax.broadcasted_iota(jnp.int32, (c, 1), 0).astype(F32)

    cos = cos_ref[...]
    sin = sin_ref[...]
    q1 = proj_ref[0, :, 0:128].astype(F32)
    q2 = proj_ref[0, :, 128:256].astype(F32)
    k1 = proj_ref[0, :, 256:384].astype(F32)
    k2 = proj_ref[0, :, 384:512].astype(F32)
    rq1 = q1 * cos - q2 * sin
    rq2 = q1 * sin + q2 * cos
    k_scale = RET_DK ** -0.5
    rk1 = (k1 * cos - k2 * sin) * k_scale
    rk2 = (k1 * sin + k2 * cos) * k_scale
    k_all = jnp.concatenate([rk1, rk2], axis=-1)
    k_all_b = k_all.astype(BF16)
    dij = (row_i - col_j).astype(F32)
    lane_head = lane // (RET_DK // 2)
    for h in range(RET_HEADS):
        log_g = math.log1p(-(2.0 ** (-5 - h)))
        hm = lane_head == h
        qm = jnp.concatenate([jnp.where(hm, rq1, 0.0), jnp.where(hm, rq2, 0.0)],
                             axis=-1).astype(BF16)
        decay = jnp.exp(jnp.where(causal, dij * log_g, -jnp.inf))
        p = (_dot_nt(qm, k_all_b) * decay).astype(BF16)
        v_h = proj_ref[0, :, 512 + 128 * h:640 + 128 * h]
        s_h = ret_state[:, 128 * h:128 * h + 128].astype(BF16)
        o = _dot(p, v_h) + _dot(qm, s_h) * jnp.exp((pos + 1.0) * log_g)
        mu = jnp.mean(o, axis=-1, keepdims=True)
        d = o - mu
        var = jnp.mean(d * d, axis=-1, keepdims=True)
        on = d * lax.rsqrt(var + EPS) * gnw_ref[:, 128 * h:128 * h + 128]
        g = proj_ref[0, :, 1024 + 128 * h:1152 + 128 * h].astype(F32)
        yret_ref[0, :, 128 * h:128 * h + 128] = (on * (g * _sigmoid(g))).astype(yret_ref.dtype)
    lane2 = lax.broadcasted_iota(jnp.int32, (1, 2 * V7X_LANES), 1)
    lane2_head = (lane2 % V7X_LANES) // (RET_DK // 2)
    row2 = lax.broadcasted_iota(jnp.int32, (2 * V7X_LANES, 1), 0)
    row2_head = (row2 % V7X_LANES) // (RET_DK // 2)
    log_g_lane = jnp.zeros((1, 2 * V7X_LANES), F32)
    log_g_row = jnp.zeros((2 * V7X_LANES, 1), F32)
    for h in range(RET_HEADS):
        log_g = math.log1p(-(2.0 ** (-5 - h)))
        log_g_lane = jnp.where(lane2_head == h, log_g, log_g_lane)
        log_g_row = jnp.where(row2_head == h, log_g, log_g_row)
    k_out = (k_all * jnp.exp((float(c - 1) - pos) * log_g_lane)).astype(BF16)
    v_all = proj_ref[0, :, 512:1024]
    ret_state[...] = ret_state[...] * jnp.exp(float(c) * log_g_row) + _dot_tn(k_out, v_all)

    conv_buf[8:8 + c, :] = proj_ref[0, :, 2048:2816].astype(F32)
    acc = jnp.zeros((c, SSD_CONV_DIM), F32) + convb_ref[...]
    for k in range(SSD_CONV):
        off = 8 - (SSD_CONV - 1) + k
        acc = acc + convw_ref[k:k + 1, :] * conv_buf[off:off + c, :]
    conv_buf[0:8, :] = conv_buf[c:c + 8, :]
    xbc = acc * _sigmoid(acc)
    xs = xbc[:, 0:SSD_INNER]
    bm = xbc[:, 512:640]
    cm = xbc[:, 640:768]

    dt = _softplus(gate_ref[0] + dtb_ref[...])
    la = -dt * jnp.exp(alog_ref[...])
    dt_t = _softplus(gate_t_ref[0] + dtb_t_ref[...])
    la_t = -dt_t * jnp.exp(alog_t_ref[...])
    tri = jnp.where(causal, 1.0, 0.0).astype(BF16)
    tri_t = jnp.where(row_i <= col_j, 1.0, 0.0).astype(BF16)
    cum = _tri_dot(tri, la)
    cum_t = _dot_tri(la_t, tri_t)
    cum_last = cum[c - 1:c, :]
    lane_half = lane // SSD_HEAD_DIM
    y_pairs = []
    for g in range(SSD_GROUPS):
        gm = lane_half == g
        cg = jnp.where(gm, cm, 0.0)
        bg = jnp.where(gm, bm, 0.0)
        cg_b = cg.astype(BF16)
        gmat = _dot_nt(cg_b, bg.astype(BF16))
        heads_per_group = SSD_HEADS // SSD_GROUPS
        for hh in range(heads_per_group):
            h = g * heads_per_group + hh
            pair, half = h // 2, h % 2
            x_pair = xs[:, 128 * pair:128 * pair + 128]
            xm = jnp.where(lane_half == half, x_pair, 0.0).astype(BF16)
            cc = cum[:, h:h + 1]
            cr = cum_t[h:h + 1, :]
            lmat = jnp.exp(jnp.where(causal, cc - cr, -jnp.inf)) * dt_t[h:h + 1, :]
            p = (gmat * lmat).astype(BF16)
            s_h = ssd_state[h]
            y = _dot(p, xm) + _dot((cg * jnp.exp(cc)).astype(BF16), s_h.astype(BF16))
            cl = cum_last[:, h:h + 1]
            k_out = (bg * (dt[:, h:h + 1] * jnp.exp(cl - cc))).astype(BF16)
            ssd_state[h] = s_h * jnp.exp(cl) + _dot_tn(k_out, xm)
            if half == 0:
                y_pairs.append(y)
            else:
                y_pairs[pair] = y_pairs[pair] + y
    y = jnp.concatenate(y_pairs, axis=-1) + dskip_ref[...] * xs
    z = proj_ref[0, :, 1536:2048].astype(F32)
    y = y * (z * _sigmoid(z))
    group = SSD_INNER // SSD_GROUPS
    for g in range(SSD_GROUPS):
        yg = y[:, group * g:group * (g + 1)]
        ones = jnp.ones((1, group), F32)
        yssd_ref[0, :, group * g:group * (g + 1)] = (
            _rms(yg, ones) * normw_ref[:, group * g:group * (g + 1)]).astype(yssd_ref.dtype)


def _even_mixer(proj, gates, gates_t, cos, sin, gn_w, conv_w, conv_b, dt_b, dt_b_t,
                a_log, a_log_t, d_skip, norm_w):
    b, t_len, _ = proj.shape
    c = CHUNK
    half = RET_HEADS * RET_DV
    return pl.pallas_call(
        _even_mixer_kernel,
        grid=(b, t_len // c),
        in_specs=[
            pl.BlockSpec((1, c, AB_MAIN), lambda i, j: (i, j, 0)),
            pl.BlockSpec((1, c, GATE_PAD), lambda i, j: (i, j, 0)),
            pl.BlockSpec((1, 8, c), lambda i, j: (i, 0, j)),
            pl.BlockSpec((c, V7X_LANES), lambda i, j: (j, 0)),
            pl.BlockSpec((c, V7X_LANES), lambda i, j: (j, 0)),
            _const_spec(gn_w.shape), _const_spec(conv_w.shape), _const_spec(conv_b.shape),
            _const_spec(dt_b.shape), _const_spec(dt_b_t.shape),
            _const_spec(a_log.shape), _const_spec(a_log_t.shape),
            _const_spec(d_skip.shape), _const_spec(norm_w.shape),
        ],
        out_specs=[
            pl.BlockSpec((1, c, half), lambda i, j: (i, j, 0)),
            pl.BlockSpec((1, c, SSD_INNER), lambda i, j: (i, j, 0)),
        ],
        out_shape=[
            jax.ShapeDtypeStruct((b, t_len, half), BF16),
            jax.ShapeDtypeStruct((b, t_len, SSD_INNER), BF16),
        ],
        scratch_shapes=[
            pltpu.VMEM((2 * V7X_LANES, RET_HEADS * RET_DV), F32),
            pltpu.VMEM((SSD_HEADS, V7X_LANES, V7X_LANES), F32),
            pltpu.VMEM((c + 8, SSD_CONV_DIM), F32),
        ],
        compiler_params=_params(("parallel", "arbitrary")),
        name="even_mixer",
    )(proj, gates, gates_t, cos, sin, gn_w, conv_w, conv_b, dt_b, dt_b_t,
      a_log, a_log_t, d_skip, norm_w)


def _hgrn2_kernel(proj_ref, lbl_ref, normw_ref, y_ref, state_t, *, layer):
    c = CHUNK
    t = pl.program_id(1)

    @pl.when(t == 0)
    def _():
        state_t[...] = jnp.zeros_like(state_t)

    logits = lbl_ref[...]
    e = jnp.exp(logits - jnp.max(logits, axis=0, keepdims=True))
    prob = e / jnp.sum(e, axis=0, keepdims=True)
    lb = jnp.zeros((1, HG_HEADS * HG_DIM), F32)
    for i in range(1, layer + 1):
        lb = lb + prob[i:i + 1, :]

    row_i = lax.broadcasted_iota(jnp.int32, (c, c), 0)
    col_j = lax.broadcasted_iota(jnp.int32, (c, c), 1)
    tri = jnp.where(row_i >= col_j, 1.0, 0.0).astype(BF16)

    width = HG_HEADS * HG_DIM
    zf = proj_ref[0, :, width:2 * width].astype(F32)
    f = lb + (1.0 - lb) * _sigmoid(zf)
    k_all = (1.0 - lb) * _sigmoid(-zf)
    cum_all = _tri_dot(tri, jnp.log(f))

    key_row = lax.broadcasted_iota(jnp.int32, (c, 1), 0)
    for h in range(HG_HEADS):
        sl = slice(HG_DIM * h, HG_DIM * (h + 1))
        q = proj_ref[0, :, HG_DIM * h:HG_DIM * (h + 1)].astype(F32)
        k = k_all[:, sl]
        cum = cum_all[:, sl]
        v = proj_ref[0, :, 2 * width + HG_DIM * h:2 * width + HG_DIM * (h + 1)]
        st = state_t[h]
        inter = _dot_nt((q * jnp.exp(cum)).astype(BF16), st.astype(BF16))
        blocks = []
        for blk in range(c // SUB):
            lo, hi = blk * SUB, (blk + 1) * SUB
            if blk == 0:
                ref_row = jnp.zeros((1, HG_DIM), F32)
            else:
                ref_row = cum[lo - 1:lo, :]
            qb = (q[lo:hi, :] * jnp.exp(cum[lo:hi, :] - ref_row)).astype(BF16)
            kb = (k * jnp.exp(jnp.where(key_row < hi, ref_row - cum, -jnp.inf))).astype(BF16)
            blocks.append(_dot_nt(qb, kb))
        scores = jnp.where(row_i >= col_j, jnp.concatenate(blocks, axis=0), 0.0)
        o = inter + _dot(scores.astype(BF16), v)
        cl = cum[c - 1:c, :]
        k_out = (k * jnp.exp(cl - cum)).astype(BF16)
        state_t[h] = st * jnp.exp(cl) + _dot_tn(v, k_out)
        g = proj_ref[0, :, 3 * width + HG_DIM * h:3 * width + HG_DIM * (h + 1)].astype(F32)
        y_ref[0, :, sl] = (_rms(o, normw_ref[...]) * (g * _sigmoid(g))).astype(y_ref.dtype)


def _hgrn2(proj, lb_logits, norm_w, layer):
    b, t_len, _ = proj.shape
    c = CHUNK
    width = HG_HEADS * HG_DIM
    kern = functools.partial(_hgrn2_kernel, layer=layer)
    return pl.pallas_call(
        kern,
        grid=(b, t_len // c),
        in_specs=[
            pl.BlockSpec((1, c, 4 * width), lambda i, j: (i, j, 0)),
            _const_spec(lb_logits.shape),
            _const_spec(norm_w.shape),
        ],
        out_specs=pl.BlockSpec((1, c, width), lambda i, j: (i, j, 0)),
        out_shape=jax.ShapeDtypeStruct((b, t_len, width), BF16),
        scratch_shapes=[pltpu.VMEM((HG_HEADS, HG_DIM, HG_DIM), F32)],
        compiler_params=_params(("parallel", "arbitrary")),
        name="hgrn2",
    )(proj, lb_logits, norm_w)


def _forget_cumsum_kernel(gate_t_ref, bias_t_ref, out_ref):
    t_len = gate_t_ref.shape[2]
    blk = V7X_LANES
    x = gate_t_ref[0] + bias_t_ref[...]
    log_f = jnp.minimum(x, 0.0) - jnp.log1p(jnp.exp(-jnp.abs(x)))
    row_i = lax.broadcasted_iota(jnp.int32, (blk, blk), 0)
    col_j = lax.broadcasted_iota(jnp.int32, (blk, blk), 1)
    tri_t = jnp.where(row_i <= col_j, 1.0, 0.0).astype(BF16)
    carry = jnp.zeros((8, 1), F32)
    for i in range(t_len // blk):
        cum = _dot_tri(log_f[:, i * blk:(i + 1) * blk], tri_t) + carry
        out_ref[0, :, i * blk:(i + 1) * blk] = cum
        carry = cum[:, blk - 1:blk]


def _forget_cumsum(gates_t, bias_t):
    b, _, t_len = gates_t.shape
    return pl.pallas_call(
        _forget_cumsum_kernel,
        grid=(b,),
        in_specs=[pl.BlockSpec((1, 8, t_len), lambda i: (i, 0, 0)), _const_spec(bias_t.shape)],
        out_specs=pl.BlockSpec((1, 8, t_len), lambda i: (i, 0, 0)),
        out_shape=jax.ShapeDtypeStruct((b, 8, t_len), F32),
        compiler_params=_params(("parallel",)),
        name="forget_cumsum",
    )(gates_t, bias_t)


def _fox_kernel(q_ref, k_ref, v_ref, f_ref, y_ref):
    sub, kb = ATT_SUB, ATT_KEYS
    n_sub = ATT_ROWS // sub
    qi = pl.program_id(2)
    q0 = pl.multiple_of(qi * ATT_ROWS, ATT_ROWS)
    f_q0 = f_ref[0, :, pl.ds(q0, V7X_LANES)][:, 0:1]

    def tile(state, r, k, v, bias, mask):
        m, l, acc = state
        s = _dot_nt(q_ref[0, r * sub:(r + 1) * sub, :], k) + bias
        if mask is not None:
            s = jnp.where(mask, s, -jnp.inf)
        m_new = jnp.maximum(m, jnp.max(s, axis=-1, keepdims=True))
        alpha = jnp.exp(m - m_new)
        p = jnp.exp(s - m_new)
        l = alpha * l + jnp.sum(p, axis=-1, keepdims=True)
        acc = alpha * acc + _dot(p.astype(BF16), v)
        return m_new, l, acc

    def load(k0, width):
        return (k_ref[0, pl.ds(k0, width), :], v_ref[0, pl.ds(k0, width), :],
                f_q0 - f_ref[0, :, pl.ds(k0, width)])

    def before(kj, states):
        k, v, bias = load(pl.multiple_of(kj * kb, kb), kb)
        return tuple(tile(states[r], r, k, v, bias, None) for r in range(n_sub))

    init = tuple((jnp.full((sub, 1), -jnp.inf, F32), jnp.zeros((sub, 1), F32),
                  jnp.zeros((sub, FOX_DIM), F32)) for _ in range(n_sub))
    states = list(lax.fori_loop(0, qi * (ATT_ROWS // kb), before, init))

    row_i = lax.broadcasted_iota(jnp.int32, (sub, sub), 0)
    col_j = lax.broadcasted_iota(jnp.int32, (sub, sub), 1)
    causal = row_i >= col_j
    for c in range(n_sub):
        k, v, bias = load(pl.multiple_of(q0 + c * sub, sub), sub)
        for r in range(c, n_sub):
            states[r] = tile(states[r], r, k, v, bias, causal if r == c else None)
    for r in range(n_sub):
        _, l, acc = states[r]
        y_ref[0, r * sub:(r + 1) * sub, :] = (acc / l).astype(y_ref.dtype)


def _fox(proj, f_cum):
    b, t_len, _ = proj.shape
    blk = ATT_ROWS
    qcol, kcol, vcol = (2048 // FOX_DIM, 2560 // FOX_DIM, 3072 // FOX_DIM)
    return pl.pallas_call(
        _fox_kernel,
        grid=(b, FOX_HEADS, t_len // blk),
        in_specs=[
            pl.BlockSpec((1, blk, FOX_DIM), lambda i, h, j: (i, j, qcol + h)),
            pl.BlockSpec((1, t_len, FOX_DIM), lambda i, h, j: (i, 0, kcol + h)),
            pl.BlockSpec((1, t_len, FOX_DIM), lambda i, h, j: (i, 0, vcol + h)),
            pl.BlockSpec((1, 1, t_len), lambda i, h, j: (i * FOX_HEADS + h, 0, 0)),
        ],
        out_specs=pl.BlockSpec((1, blk, FOX_DIM), lambda i, h, j: (i, j, h)),
        out_shape=jax.ShapeDtypeStruct((b, t_len, FOX_HEADS * FOX_DIM), BF16),
        compiler_params=_params(("parallel", "parallel", "arbitrary")),
        name="fox_attention",
    )(proj, proj, proj, f_cum)


def _rotary_perm():
    perm = np.zeros((RET_HEADS * RET_DK,), np.int32)
    for half in range(2):
        for h in range(RET_HEADS):
            for i in range(RET_DK // 2):
                perm[half * 128 + h * 32 + i] = h * RET_DK + 2 * i + half
    return perm


def _pad_lanes(a, width=GATE_PAD):
    return jnp.pad(a, [(0, 0)] * (a.ndim - 1) + [(0, width - a.shape[-1])])


def kernel(x, norm_mix, norm_ffn, ffn_w_gate, ffn_w_up, ffn_w_down, ab_w_in, ab_w_out, ret_gn_w,
           ssd_conv_w, ssd_conv_b, ssd_dt_bias, ssd_a_log, ssd_d, ssd_norm_w, cd_w_in, cd_w_out,
           hg_lb_logits, hg_norm_w, fox_f_bias, fox_q_norm_w, fox_k_norm_w):
    b, t_len, d = x.shape
    depth = norm_mix.shape[0]
    n_rows = b * t_len
    perm = _rotary_perm()

    half = RET_DK // 2
    freqs = ROPE_BASE ** (-jnp.linspace(0.0, 1.0, half, dtype=F32))
    ang = jnp.arange(t_len, dtype=F32)[:, None] * freqs[None, :]
    cos = jnp.tile(jnp.cos(ang), (1, RET_HEADS))
    sin = jnp.tile(jnp.sin(ang), (1, RET_HEADS))

    h2d = x.reshape(n_rows, d)
    for layer in range(depth):
        j = layer // 2
        if layer % 2 == 0:
            w = ab_w_in[j]
            w_main = jnp.concatenate(
                [w[:, 0:256][:, perm], w[:, 256:512][:, perm], w[:, 512:AB_MAIN]], axis=1).astype(BF16)
            w_gate = _pad_lanes(w[:, AB_MAIN:AB_MAIN + AB_GATES]).astype(BF16)
            dummy = jnp.ones((1, V7X_LANES), F32)
            proj, gates = _inproj(h2d, norm_mix[layer][None, :], w_main, w_gate, dummy, (0, 0))
            gates3 = gates.reshape(b, t_len, GATE_PAD)
            gates_t = jnp.transpose(gates3[:, :, 0:8], (0, 2, 1))
            y_a, y_b = _even_mixer(
                proj.reshape(b, t_len, AB_MAIN), gates3, gates_t, cos, sin,
                ret_gn_w[j].reshape(1, RET_HEADS * RET_DV),
                ssd_conv_w[j], ssd_conv_b[j][None, :],
                _pad_lanes(ssd_dt_bias[j][None, :]), ssd_dt_bias[j][:, None],
                _pad_lanes(ssd_a_log[j][None, :]), ssd_a_log[j][:, None],
                jnp.repeat(ssd_d[j], SSD_HEAD_DIM)[None, :], ssd_norm_w[j][None, :])
            w_out = ab_w_out[j]
        else:
            w = cd_w_in[j]
            w_main = w[:, 0:CD_MAIN].astype(BF16)
            w_gate = _pad_lanes(w[:, CD_MAIN:CD_MAIN + CD_GATES]).astype(BF16)
            qk_w = jnp.concatenate([jnp.tile(fox_q_norm_w[j] * (FOX_DIM ** -0.5), FOX_HEADS),
                                    jnp.tile(fox_k_norm_w[j], FOX_HEADS)])[None, :]
            proj, gates = _inproj(h2d, norm_mix[layer][None, :], w_main, w_gate, qk_w, (2048, 3072))
            proj3 = proj.reshape(b, t_len, CD_MAIN)
            gates_t = jnp.transpose(gates.reshape(b, t_len, GATE_PAD)[:, :, 0:8], (0, 2, 1))
            bias_t = jnp.pad(fox_f_bias[j], (0, 8 - FOX_HEADS))[:, None]
            f_cum = _forget_cumsum(gates_t, bias_t)[:, 0:FOX_HEADS, :].reshape(b * FOX_HEADS, 1, t_len)
            y_a = _hgrn2(proj3, hg_lb_logits, hg_norm_w[j][None, :], j)
            y_b = _fox(proj3, f_cum)
            w_out = cd_w_out[j]
        half_out = w_out.shape[0] // 2
        h2d = _outproj_ffn(
            y_a.reshape(n_rows, half_out), y_b.reshape(n_rows, half_out), h2d,
            w_out[0:half_out].astype(BF16), w_out[half_out:].astype(BF16),
            norm_ffn[layer][None, :],
            ffn_w_gate[layer].astype(BF16), ffn_w_up[layer].astype(BF16), ffn_w_down[layer].astype(BF16))
    return h2d.reshape(b, t_len, d)
```

```python
import functools
import math

import numpy as np
import jax
import jax.numpy as jnp
from jax import lax
from jax.experimental import pallas as pl
from jax.experimental.pallas import tpu as pltpu

F32 = jnp.float32
BF16 = jnp.bfloat16

D_MODEL = 1024
EPS = 1e-6
ROPE_BASE = 10000.0
LOG2_E = 1.4426950408889634

RET_HEADS, RET_DK, RET_DV = 4, 64, 128
SSD_HEADS, SSD_HEAD_DIM, SSD_GROUPS, SSD_STATE, SSD_CONV = 8, 64, 2, 64, 4
SSD_INNER = SSD_HEADS * SSD_HEAD_DIM
SSD_CONV_DIM = SSD_INNER + 2 * SSD_GROUPS * SSD_STATE
HG_HEADS, HG_DIM = 4, 128
FOX_HEADS, FOX_DIM = 4, 128
FFN_HIDDEN = 2816

AB_MAIN = 2816
AB_GATES = SSD_HEADS
CD_MAIN = 3584
CD_GATES = FOX_HEADS
GATE_PAD = 128

V7X_LANES = 128
V7X_VMEM_LIMIT = 56 * 1024 * 1024

ROW_TILE = 512
COL_TILE = 512
FFN_TILE = 256
CHUNK = 128
SUB = 16
ATT_ROWS = 1024
ATT_SUB = 256
ATT_KEYS = 1024


def _dot(a, b):
    return jnp.dot(a, b, preferred_element_type=F32)


def _dot_nt(a, b):
    return lax.dot_general(a, b, (((1,), (1,)), ((), ())), preferred_element_type=F32)


def _dot_tn(a, b):
    return lax.dot_general(a, b, (((0,), (0,)), ((), ())), preferred_element_type=F32)


def _split3(x):
    hi = x.astype(BF16)
    r = x - hi.astype(F32)
    mid = r.astype(BF16)
    lo = (r - mid.astype(F32)).astype(BF16)
    return hi, mid, lo


def _tri_dot(tri, x):
    hi, mid, lo = _split3(x)
    return _dot(tri, hi) + _dot(tri, mid) + _dot(tri, lo)


def _dot_tri(x, tri):
    hi, mid, lo = _split3(x)
    return _dot(hi, tri) + _dot(mid, tri) + _dot(lo, tri)


def _sigmoid(x):
    return 1.0 / (1.0 + jnp.exp(-x))


def _softplus(x):
    return jnp.maximum(x, 0.0) + jnp.log1p(jnp.exp(-jnp.abs(x)))


def _rms(x, w):
    return x * lax.rsqrt(jnp.mean(x * x, axis=-1, keepdims=True) + EPS) * w


def _const_spec(shape):
    zeros = (0,) * len(shape)
    return pl.BlockSpec(shape, lambda *_: zeros, pipeline_mode=pl.Buffered(1))


def _params(semantics):
    return pltpu.CompilerParams(dimension_semantics=semantics,
                                vmem_limit_bytes=V7X_VMEM_LIMIT)


def _inproj_kernel(h_ref, nw_ref, w_ref, wqk_ref, qkw_ref, out_ref, gate_ref, wb_ref, *,
                   n_main, norm_cols):
    n_all = w_ref.shape[2]

    @pl.when(pl.program_id(0) == 0)
    def _():
        for c0 in range(0, n_main, COL_TILE):
            width = min(COL_TILE, n_main - c0)
            wb_ref[:, c0:c0 + width] = w_ref[0, :, c0:c0 + width].astype(BF16)
        wb_ref[:, n_main:n_main + GATE_PAD] = jnp.zeros((D_MODEL, GATE_PAD), BF16)
        wb_ref[:, n_main:n_all] = w_ref[0, :, n_main:n_all].astype(BF16)
        if wqk_ref is not None:
            wb_ref[:, 0:wqk_ref.shape[1]] = wqk_ref[...].astype(BF16)

    x = h_ref[...]
    ub = _rms(x, nw_ref[...]).astype(BF16)
    for c0 in range(0, n_main, COL_TILE):
        width = min(COL_TILE, n_main - c0)
        y = _dot(ub, wb_ref[:, c0:c0 + width])
        for l0 in range(0, width, V7X_LANES):
            col = c0 + l0
            piece = y[:, l0:l0 + V7X_LANES]
            if norm_cols[0] <= col < norm_cols[1]:
                piece = _rms(piece, qkw_ref[:, col - norm_cols[0]:col - norm_cols[0] + V7X_LANES])
            out_ref[:, col:col + V7X_LANES] = piece.astype(out_ref.dtype)
    gate_ref[...] = _dot(ub, wb_ref[:, n_main:n_main + GATE_PAD])


def _inproj(h2d, norm_w, w_all, layer, n_main, w_qk=None, qk_w=None, norm_cols=(0, 0)):
    n_rows = h2d.shape[0]
    n_all = w_all.shape[2]

    def body(h_ref, nw_ref, w_ref, *rest):
        rest = list(rest)
        wqk_ref = rest.pop(0) if w_qk is not None else None
        qkw_ref = rest.pop(0) if qk_w is not None else None
        _inproj_kernel(h_ref, nw_ref, w_ref, wqk_ref, qkw_ref, *rest,
                       n_main=n_main, norm_cols=norm_cols)

    operands = [h2d, norm_w, w_all]
    in_specs = [
        pl.BlockSpec((ROW_TILE, D_MODEL), lambda i: (i, 0)),
        _const_spec((1, D_MODEL)),
        pl.BlockSpec((1, D_MODEL, n_all), lambda i: (layer, 0, 0), pipeline_mode=pl.Buffered(1)),
    ]
    for extra in (w_qk, qk_w):
        if extra is not None:
            operands.append(extra)
            in_specs.append(_const_spec(extra.shape))
    return pl.pallas_call(
        body,
        grid=(n_rows // ROW_TILE,),
        in_specs=in_specs,
        out_specs=[
            pl.BlockSpec((ROW_TILE, n_main), lambda i: (i, 0)),
            pl.BlockSpec((ROW_TILE, GATE_PAD), lambda i: (i, 0)),
        ],
        out_shape=[
            jax.ShapeDtypeStruct((n_rows, n_main), BF16),
            jax.ShapeDtypeStruct((n_rows, GATE_PAD), F32),
        ],
        scratch_shapes=[pltpu.VMEM((D_MODEL, n_main + GATE_PAD), BF16)],
        compiler_params=_params(("arbitrary",)),
        name="inproj",
    )(*operands)


def _ffn_kernel(ya_ref, yb_ref, h_ref, woa_ref, wob_ref, nw_ref, wg_ref, wu_ref, wd_ref,
                out_ref, act_ref):
    h1 = h_ref[...] + _dot(ya_ref[...], woa_ref[0]) + _dot(yb_ref[...], wob_ref[0])
    ub = _rms(h1, nw_ref[...]).astype(BF16)
    for c0 in range(0, FFN_HIDDEN, FFN_TILE):
        g = _dot(ub, wg_ref[0, :, c0:c0 + FFN_TILE])
        up = _dot(ub, wu_ref[0, :, c0:c0 + FFN_TILE])
        act_ref[:, c0:c0 + FFN_TILE] = (g * _sigmoid(g) * up).astype(BF16)
    out_ref[...] = h1 + _dot(act_ref[...], wd_ref[0])


def _layer_spec(shape, layer, row_block=0):
    return pl.BlockSpec((1,) + shape, lambda i: (layer, row_block, 0), pipeline_mode=pl.Buffered(1))


def _outproj_ffn(ya, yb, h2d, w_out, j, norm_w, w_gate, w_up, w_down, layer):
    n_rows = h2d.shape[0]
    half = ya.shape[1]
    return pl.pallas_call(
        _ffn_kernel,
        grid=(n_rows // ROW_TILE,),
        in_specs=[
            pl.BlockSpec((ROW_TILE, half), lambda i: (i, 0)),
            pl.BlockSpec((ROW_TILE, half), lambda i: (i, 0)),
            pl.BlockSpec((ROW_TILE, D_MODEL), lambda i: (i, 0)),
            _layer_spec((half, D_MODEL), j, 0),
            _layer_spec((half, D_MODEL), j, 1),
            _const_spec((1, D_MODEL)),
            _layer_spec((D_MODEL, FFN_HIDDEN), layer),
            _layer_spec((D_MODEL, FFN_HIDDEN), layer),
            _layer_spec((FFN_HIDDEN, D_MODEL), layer),
        ],
        out_specs=pl.BlockSpec((ROW_TILE, D_MODEL), lambda i: (i, 0)),
        out_shape=jax.ShapeDtypeStruct((n_rows, D_MODEL), F32),
        scratch_shapes=[pltpu.VMEM((ROW_TILE, FFN_HIDDEN), BF16)],
        compiler_params=_params(("parallel",)),
        name="outproj_ffn",
    )(ya, yb, h2d, w_out, w_out, norm_w, w_gate, w_up, w_down)


def _even_mixer_kernel(proj_ref, gate_ref, gate_t_ref, cos_ref, sin_ref, gnw_ref,
                       convw_ref, convb_ref, dtb_ref, dtb_t_ref, alog_ref, alog_t_ref,
                       dskip_ref, normw_ref,
                       yret_ref, yssd_ref,
                       ret_state, ssd_state, conv_buf):
    c = CHUNK
    t = pl.program_id(1)

    @pl.when(t == 0)
    def _():
        ret_state[...] = jnp.zeros_like(ret_state)
        ssd_state[...] = jnp.zeros_like(ssd_state)
        conv_buf[0:8, :] = jnp.zeros((8, SSD_CONV_DIM), F32)

    row_i = lax.broadcasted_iota(jnp.int32, (c, c), 0)
    col_j = lax.broadcasted_iota(jnp.int32, (c, c), 1)
    causal = row_i >= col_j
    lane = lax.broadcasted_iota(jnp.int32, (1, V7X_LANES), 1)
    pos = lax.broadcasted_iota(jnp.int32, (c, 1), 0).astype(F32)

    cos = cos_ref[...]
    sin = sin_ref[...]
    q1 = proj_ref[0, :, 0:128].astype(F32)
    q2 = proj_ref[0, :, 128:256].astype(F32)
    k1 = proj_ref[0, :, 256:384].astype(F32)
    k2 = proj_ref[0, :, 384:512].astype(F32)
    rq1 = q1 * cos - q2 * sin
    rq2 = q1 * sin + q2 * cos
    k_scale = RET_DK ** -0.5
    rk1 = (k1 * cos - k2 * sin) * k_scale
    rk2 = (k1 * sin + k2 * cos) * k_scale
    k_all = jnp.concatenate([rk1, rk2], axis=-1)
    k_all_b = k_all.astype(BF16)
    dij = (row_i - col_j).astype(F32)
    lane_head = lane // (RET_DK // 2)
    for h in range(RET_HEADS):
        log_g = math.log1p(-(2.0 ** (-5 - h)))
        hm = lane_head == h
        qm = jnp.concatenate([jnp.where(hm, rq1, 0.0), jnp.where(hm, rq2, 0.0)],
                             axis=-1).astype(BF16)
        decay = jnp.exp(jnp.where(causal, dij * log_g, -jnp.inf))
        p = (_dot_nt(qm, k_all_b) * decay).astype(BF16)
        v_h = proj_ref[0, :, 512 + 128 * h:640 + 128 * h]
        s_h = ret_state[:, 128 * h:128 * h + 128].astype(BF16)
        o = _dot(p, v_h) + _dot(qm, s_h) * jnp.exp((pos + 1.0) * log_g)
        mu = jnp.mean(o, axis=-1, keepdims=True)
        d = o - mu
        var = jnp.mean(d * d, axis=-1, keepdims=True)
        on = d * lax.rsqrt(var + EPS) * gnw_ref[:, 128 * h:128 * h + 128]
        g = proj_ref[0, :, 1024 + 128 * h:1152 + 128 * h].astype(F32)
        yret_ref[0, :, 128 * h:128 * h + 128] = (on * (g * _sigmoid(g))).astype(yret_ref.dtype)
    lane2 = lax.broadcasted_iota(jnp.int32, (1, 2 * V7X_LANES), 1)
    lane2_head = (lane2 % V7X_LANES) // (RET_DK // 2)
    row2 = lax.broadcasted_iota(jnp.int32, (2 * V7X_LANES, 1), 0)
    row2_head = (row2 % V7X_LANES) // (RET_DK // 2)
    log_g_lane = jnp.zeros((1, 2 * V7X_LANES), F32)
    log_g_row = jnp.zeros((2 * V7X_LANES, 1), F32)
    for h in range(RET_HEADS):
        log_g = math.log1p(-(2.0 ** (-5 - h)))
        log_g_lane = jnp.where(lane2_head == h, log_g, log_g_lane)
        log_g_row = jnp.where(row2_head == h, log_g, log_g_row)
    k_out = (k_all * jnp.exp((float(c - 1) - pos) * log_g_lane)).astype(BF16)
    v_all = proj_ref[0, :, 512:1024]
    ret_state[...] = ret_state[...] * jnp.exp(float(c) * log_g_row) + _dot_tn(k_out, v_all)

    conv_buf[8:8 + c, :] = proj_ref[0, :, 2048:2816].astype(F32)
    acc = jnp.zeros((c, SSD_CONV_DIM), F32) + convb_ref[...]
    for k in range(SSD_CONV):
        off = 8 - (SSD_CONV - 1) + k
        acc = acc + convw_ref[k:k + 1, :] * conv_buf[off:off + c, :]
    conv_buf[0:8, :] = conv_buf[c:c + 8, :]
    xbc = acc * _sigmoid(acc)
    xs = xbc[:, 0:SSD_INNER]
    bm = xbc[:, 512:640]
    cm = xbc[:, 640:768]

    dt = _softplus(gate_ref[0] + dtb_ref[...])
    la = -dt * jnp.exp(alog_ref[...])
    dt_t = _softplus(gate_t_ref[0] + dtb_t_ref[...])
    la_t = -dt_t * jnp.exp(alog_t_ref[...])
    tri = jnp.where(causal, 1.0, 0.0).astype(BF16)
    tri_t = jnp.where(row_i <= col_j, 1.0, 0.0).astype(BF16)
    cum = _tri_dot(tri, la)
    cum_t = _dot_tri(la_t, tri_t)
    cum_last = cum[c - 1:c, :]
    lane_half = lane // SSD_HEAD_DIM
    y_pairs = []
    for g in range(SSD_GROUPS):
        gm = lane_half == g
        cg = jnp.where(gm, cm, 0.0)
        bg = jnp.where(gm, bm, 0.0)
        cg_b = cg.astype(BF16)
        gmat = _dot_nt(cg_b, bg.astype(BF16))
        heads_per_group = SSD_HEADS // SSD_GROUPS
        for hh in range(heads_per_group):
            h = g * heads_per_group + hh
            pair, half = h // 2, h % 2
            x_pair = xs[:, 128 * pair:128 * pair + 128]
            xm = jnp.where(lane_half == half, x_pair, 0.0).astype(BF16)
            cc = cum[:, h:h + 1]
            cr = cum_t[h:h + 1, :]
            lmat = jnp.exp(jnp.where(causal, cc - cr, -jnp.inf)) * dt_t[h:h + 1, :]
            p = (gmat * lmat).astype(BF16)
            s_h = ssd_state[h]
            y = _dot(p, xm) + _dot((cg * jnp.exp(cc)).astype(BF16), s_h.astype(BF16))
            cl = cum_last[:, h:h + 1]
            k_out = (bg * (dt[:, h:h + 1] * jnp.exp(cl - cc))).astype(BF16)
            ssd_state[h] = s_h * jnp.exp(cl) + _dot_tn(k_out, xm)
            if half == 0:
                y_pairs.append(y)
            else:
                y_pairs[pair] = y_pairs[pair] + y
    y = jnp.concatenate(y_pairs, axis=-1) + dskip_ref[...] * xs
    z = proj_ref[0, :, 1536:2048].astype(F32)
    y = y * (z * _sigmoid(z))
    group = SSD_INNER // SSD_GROUPS
    for g in range(SSD_GROUPS):
        yg = y[:, group * g:group * (g + 1)]
        ones = jnp.ones((1, group), F32)
        yssd_ref[0, :, group * g:group * (g + 1)] = (
            _rms(yg, ones) * normw_ref[:, group * g:group * (g + 1)]).astype(yssd_ref.dtype)


def _even_mixer(proj, gates, gates_t, cos, sin, gn_w, conv_w, conv_b, dt_b, dt_b_t,
                a_log, a_log_t, d_skip, norm_w):
    b, t_len, _ = proj.shape
    c = CHUNK
    half = RET_HEADS * RET_DV
    return pl.pallas_call(
        _even_mixer_kernel,
        grid=(b, t_len // c),
        in_specs=[
            pl.BlockSpec((1, c, AB_MAIN), lambda i, j: (i, j, 0)),
            pl.BlockSpec((1, c, GATE_PAD), lambda i, j: (i, j, 0)),
            pl.BlockSpec((1, 8, c), lambda i, j: (i, 0, j)),
            pl.BlockSpec((c, V7X_LANES), lambda i, j: (j, 0)),
            pl.BlockSpec((c, V7X_LANES), lambda i, j: (j, 0)),
            _const_spec(gn_w.shape), _const_spec(conv_w.shape), _const_spec(conv_b.shape),
            _const_spec(dt_b.shape), _const_spec(dt_b_t.shape),
            _const_spec(a_log.shape), _const_spec(a_log_t.shape),
            _const_spec(d_skip.shape), _const_spec(norm_w.shape),
        ],
        out_specs=[
            pl.BlockSpec((1, c, half), lambda i, j: (i, j, 0)),
            pl.BlockSpec((1, c, SSD_INNER), lambda i, j: (i, j, 0)),
        ],
        out_shape=[
            jax.ShapeDtypeStruct((b, t_len, half), BF16),
            jax.ShapeDtypeStruct((b, t_len, SSD_INNER), BF16),
        ],
        scratch_shapes=[
            pltpu.VMEM((2 * V7X_LANES, RET_HEADS * RET_DV), F32),
            pltpu.VMEM((SSD_HEADS, V7X_LANES, V7X_LANES), F32),
            pltpu.VMEM((c + 8, SSD_CONV_DIM), F32),
        ],
        compiler_params=_params(("parallel", "arbitrary")),
        name="even_mixer",
    )(proj, gates, gates_t, cos, sin, gn_w, conv_w, conv_b, dt_b, dt_b_t,
      a_log, a_log_t, d_skip, norm_w)


def _hgrn2_kernel(proj_ref, lbl_ref, normw_ref, y_ref, state_t, *, layer):
    c = CHUNK
    t = pl.program_id(1)

    @pl.when(t == 0)
    def _():
        state_t[...] = jnp.zeros_like(state_t)

    logits = lbl_ref[...]
    e = jnp.exp(logits - jnp.max(logits, axis=0, keepdims=True))
    prob = e / jnp.sum(e, axis=0, keepdims=True)
    lb = jnp.zeros((1, HG_HEADS * HG_DIM), F32)
    for i in range(1, layer + 1):
        lb = lb + prob[i:i + 1, :]

    row_i = lax.broadcasted_iota(jnp.int32, (c, c), 0)
    col_j = lax.broadcasted_iota(jnp.int32, (c, c), 1)
    tri = jnp.where(row_i >= col_j, 1.0, 0.0).astype(BF16)

    width = HG_HEADS * HG_DIM
    zf = proj_ref[0, :, width:2 * width].astype(F32)
    f = lb + (1.0 - lb) * _sigmoid(zf)
    k_all = (1.0 - lb) * _sigmoid(-zf)
    cum_all = _tri_dot(tri, jnp.log(f))

    key_row = lax.broadcasted_iota(jnp.int32, (c, 1), 0)
    for h in range(HG_HEADS):
        sl = slice(HG_DIM * h, HG_DIM * (h + 1))
        q = proj_ref[0, :, HG_DIM * h:HG_DIM * (h + 1)].astype(F32)
        k = k_all[:, sl]
        cum = cum_all[:, sl]
        v = proj_ref[0, :, 2 * width + HG_DIM * h:2 * width + HG_DIM * (h + 1)]
        st = state_t[h]
        inter = _dot_nt((q * jnp.exp(cum)).astype(BF16), st.astype(BF16))
        blocks = []
        for blk in range(c // SUB):
            lo, hi = blk * SUB, (blk + 1) * SUB
            if blk == 0:
                ref_row = jnp.zeros((1, HG_DIM), F32)
            else:
                ref_row = cum[lo - 1:lo, :]
            qb = (q[lo:hi, :] * jnp.exp(cum[lo:hi, :] - ref_row)).astype(BF16)
            kb = (k * jnp.exp(jnp.where(key_row < hi, ref_row - cum, -jnp.inf))).astype(BF16)
            blocks.append(_dot_nt(qb, kb))
        scores = jnp.where(row_i >= col_j, jnp.concatenate(blocks, axis=0), 0.0)
        o = inter + _dot(scores.astype(BF16), v)
        cl = cum[c - 1:c, :]
        k_out = (k * jnp.exp(cl - cum)).astype(BF16)
        state_t[h] = st * jnp.exp(cl) + _dot_tn(v, k_out)
        g = proj_ref[0, :, 3 * width + HG_DIM * h:3 * width + HG_DIM * (h + 1)].astype(F32)
        y_ref[0, :, sl] = (_rms(o, normw_ref[...]) * (g * _sigmoid(g))).astype(y_ref.dtype)


def _hgrn2(proj, lb_logits, norm_w, layer):
    b, t_len, _ = proj.shape
    c = CHUNK
    width = HG_HEADS * HG_DIM
    kern = functools.partial(_hgrn2_kernel, layer=layer)
    return pl.pallas_call(
        kern,
        grid=(b, t_len // c),
        in_specs=[
            pl.BlockSpec((1, c, 4 * width), lambda i, j: (i, j, 0)),
            _const_spec(lb_logits.shape),
            _const_spec(norm_w.shape),
        ],
        out_specs=pl.BlockSpec((1, c, width), lambda i, j: (i, j, 0)),
        out_shape=jax.ShapeDtypeStruct((b, t_len, width), BF16),
        scratch_shapes=[pltpu.VMEM((HG_HEADS, HG_DIM, HG_DIM), F32)],
        compiler_params=_params(("parallel", "arbitrary")),
        name="hgrn2",
    )(proj, lb_logits, norm_w)


def _forget_bias_kernel(gate_ref, bias_ref, out_ref):
    t_len = gate_ref.shape[1]
    blk = V7X_LANES
    row_i = lax.broadcasted_iota(jnp.int32, (blk, blk), 0)
    col_j = lax.broadcasted_iota(jnp.int32, (blk, blk), 1)
    tri = jnp.where(row_i >= col_j, 1.0, 0.0).astype(BF16)
    lane = lax.broadcasted_iota(jnp.int32, (1, V7X_LANES), 1)
    offset = jnp.zeros((1, V7X_LANES), F32)
    for i in range(t_len // blk):
        x = gate_ref[0, i * blk:(i + 1) * blk, :] + bias_ref[...]
        log_f = jnp.minimum(x, 0.0) - jnp.log1p(jnp.exp(-jnp.abs(x)))
        local = _tri_dot(tri, log_f)
        hi, mid, lo = _split3((local + offset) * -LOG2_E)
        offset = offset + local[blk - 1:blk, :]
        hi, mid, lo = hi.astype(F32), mid.astype(F32), lo.astype(F32)
        for h in range(FOX_HEADS):
            parts = jnp.where(lane == 0, hi[:, h:h + 1],
                              jnp.where(lane == 1, mid[:, h:h + 1],
                                        jnp.where(lane == 2, lo[:, h:h + 1], 0.0)))
            out_ref[0, i * blk:(i + 1) * blk, h * V7X_LANES:(h + 1) * V7X_LANES] = parts.astype(BF16)


def _forget_bias(gates, bias):
    b, t_len, _ = gates.shape
    return pl.pallas_call(
        _forget_bias_kernel,
        grid=(b,),
        in_specs=[pl.BlockSpec((1, t_len, GATE_PAD), lambda i: (i, 0, 0)), _const_spec(bias.shape)],
        out_specs=pl.BlockSpec((1, t_len, FOX_HEADS * V7X_LANES), lambda i: (i, 0, 0)),
        out_shape=jax.ShapeDtypeStruct((b, t_len, FOX_HEADS * V7X_LANES), BF16),
        compiler_params=_params(("parallel",)),
        name="forget_bias",
    )(gates, bias)


def _fox_kernel(q_ref, k_ref, kb_ref, v_ref, y_ref):
    sub, keys = ATT_SUB, ATT_KEYS
    n_sub = ATT_ROWS // sub
    qi = pl.program_id(2)
    q0 = pl.multiple_of(qi * ATT_ROWS, ATT_ROWS)
    lane = lax.broadcasted_iota(jnp.int32, (sub, V7X_LANES), 1)
    q_ones = jnp.where(lane < 3, 1.0, 0.0).astype(BF16)

    def tile(state, r, k, v, mask):
        m, acc = state
        q = jnp.concatenate([q_ref[0, r * sub:(r + 1) * sub, :], q_ones], axis=-1)
        s = _dot_nt(q, k)
        if mask is not None:
            s = jnp.where(mask, s, -jnp.inf)
        m_new = jnp.maximum(m, jnp.max(s, axis=-1, keepdims=True))
        alpha = jnp.exp2(m - m_new)
        p = jnp.exp2(s - m_new)
        acc = alpha * acc + _dot(p.astype(BF16), v)
        return m_new, acc

    def load(k0, width):
        k = jnp.concatenate([k_ref[0, pl.ds(k0, width), :], kb_ref[0, pl.ds(k0, width), :]], axis=-1)
        v = jnp.concatenate([v_ref[0, pl.ds(k0, width), :], jnp.ones((width, V7X_LANES), BF16)],
                            axis=-1)
        return k, v

    def before(kj, states):
        k, v = load(pl.multiple_of(kj * keys, keys), keys)
        return tuple(tile(states[r], r, k, v, None) for r in range(n_sub))

    init = tuple((jnp.full((sub, 1), -jnp.inf, F32), jnp.zeros((sub, 2 * FOX_DIM), F32))
                 for _ in range(n_sub))
    states = list(lax.fori_loop(0, qi * (ATT_ROWS // keys), before, init))

    row_i = lax.broadcasted_iota(jnp.int32, (sub, sub), 0)
    col_j = lax.broadcasted_iota(jnp.int32, (sub, sub), 1)
    causal = row_i >= col_j
    for c in range(n_sub):
        k, v = load(pl.multiple_of(q0 + c * sub, sub), sub)
        for r in range(c, n_sub):
            states[r] = tile(states[r], r, k, v, causal if r == c else None)
    for r in range(n_sub):
        _, acc = states[r]
        y_ref[0, r * sub:(r + 1) * sub, :] = (acc[:, 0:FOX_DIM] / acc[:, FOX_DIM:]).astype(y_ref.dtype)


def _fox(proj, key_bias):
    b, t_len, _ = proj.shape
    blk = ATT_ROWS
    qcol, kcol, vcol = (2048 // FOX_DIM, 2560 // FOX_DIM, 3072 // FOX_DIM)
    return pl.pallas_call(
        _fox_kernel,
        grid=(b, FOX_HEADS, t_len // blk),
        in_specs=[
            pl.BlockSpec((1, blk, FOX_DIM), lambda i, h, j: (i, j, qcol + h)),
            pl.BlockSpec((1, t_len, FOX_DIM), lambda i, h, j: (i, 0, kcol + h)),
            pl.BlockSpec((1, t_len, V7X_LANES), lambda i, h, j: (i, 0, h)),
            pl.BlockSpec((1, t_len, FOX_DIM), lambda i, h, j: (i, 0, vcol + h)),
        ],
        out_specs=pl.BlockSpec((1, blk, FOX_DIM), lambda i, h, j: (i, j, h)),
        out_shape=jax.ShapeDtypeStruct((b, t_len, FOX_HEADS * FOX_DIM), BF16),
        compiler_params=_params(("parallel", "parallel", "arbitrary")),
        name="fox_attention",
    )(proj, proj, key_bias, proj)


def _rotary_perm():
    width = RET_HEADS * RET_DK
    perm = np.zeros((2 * width,), np.int32)
    for base in (0, width):
        for half in range(2):
            for h in range(RET_HEADS):
                for i in range(RET_DK // 2):
                    perm[base + half * 128 + h * 32 + i] = base + h * RET_DK + 2 * i + half
    return perm


def _pad_lanes(a, width=GATE_PAD):
    return jnp.pad(a, [(0, 0)] * (a.ndim - 1) + [(0, width - a.shape[-1])])


def kernel(x, norm_mix, norm_ffn, ffn_w_gate, ffn_w_up, ffn_w_down, ab_w_in, ab_w_out, ret_gn_w,
           ssd_conv_w, ssd_conv_b, ssd_dt_bias, ssd_a_log, ssd_d, ssd_norm_w, cd_w_in, cd_w_out,
           hg_lb_logits, hg_norm_w, fox_f_bias, fox_q_norm_w, fox_k_norm_w):
    b, t_len, d = x.shape
    depth = norm_mix.shape[0]
    n_rows = b * t_len
    perm = _rotary_perm()

    half = RET_DK // 2
    freqs = ROPE_BASE ** (-jnp.linspace(0.0, 1.0, half, dtype=F32))
    ang = jnp.arange(t_len, dtype=F32)[:, None] * freqs[None, :]
    cos = jnp.tile(jnp.cos(ang), (1, RET_HEADS))
    sin = jnp.tile(jnp.sin(ang), (1, RET_HEADS))

    ab_w_out_b, cd_w_out_b = ab_w_out.astype(BF16), cd_w_out.astype(BF16)
    w_gate_b, w_up_b, w_down_b = (ffn_w_gate.astype(BF16), ffn_w_up.astype(BF16),
                                  ffn_w_down.astype(BF16))

    h2d = x.reshape(n_rows, d)
    for layer in range(depth):
        j = layer // 2
        if layer % 2 == 0:
            w_qk = ab_w_in[j, :, 0:2 * RET_HEADS * RET_DK][:, perm]
            proj, gates = _inproj(h2d, norm_mix[layer][None, :], ab_w_in, j, AB_MAIN, w_qk=w_qk)
            gates3 = gates.reshape(b, t_len, GATE_PAD)
            gates_t = jnp.transpose(gates3[:, :, 0:8], (0, 2, 1))
            y_a, y_b = _even_mixer(
                proj.reshape(b, t_len, AB_MAIN), gates3, gates_t, cos, sin,
                ret_gn_w[j].reshape(1, RET_HEADS * RET_DV),
                ssd_conv_w[j], ssd_conv_b[j][None, :],
                _pad_lanes(ssd_dt_bias[j][None, :]), ssd_dt_bias[j][:, None],
                _pad_lanes(ssd_a_log[j][None, :]), ssd_a_log[j][:, None],
                jnp.repeat(ssd_d[j], SSD_HEAD_DIM)[None, :], ssd_norm_w[j][None, :])
            w_out = ab_w_out_b
        else:
            qk_w = jnp.concatenate([jnp.tile(fox_q_norm_w[j] * (FOX_DIM ** -0.5 * LOG2_E), FOX_HEADS),
                                    jnp.tile(fox_k_norm_w[j], FOX_HEADS)])[None, :]
            proj, gates = _inproj(h2d, norm_mix[layer][None, :], cd_w_in, j, CD_MAIN,
                                  qk_w=qk_w, norm_cols=(2048, 3072))
            proj3 = proj.reshape(b, t_len, CD_MAIN)
            key_bias = _forget_bias(gates.reshape(b, t_len, GATE_PAD),
                                    _pad_lanes(fox_f_bias[j][None, :]))
            y_a = _hgrn2(proj3, hg_lb_logits, hg_norm_w[j][None, :], j)
            y_b = _fox(proj3, key_bias)
            w_out = cd_w_out_b
        half_out = w_out.shape[1] // 2
        h2d = _outproj_ffn(
            y_a.reshape(n_rows, half_out), y_b.reshape(n_rows, half_out), h2d,
            w_out, j, norm_ffn[layer][None, :], w_gate_b, w_up_b, w_down_b, layer)
    return h2d.reshape(b, t_len, d)
```

```python
import functools
import math

import numpy as np
import jax
import jax.numpy as jnp
from jax import lax
from jax.experimental import pallas as pl
from jax.experimental.pallas import tpu as pltpu

F32 = jnp.float32
BF16 = jnp.bfloat16

D_MODEL = 1024
EPS = 1e-6
ROPE_BASE = 10000.0
LOG2_E = 1.4426950408889634

RET_HEADS, RET_DK, RET_DV = 4, 64, 128
SSD_HEADS, SSD_HEAD_DIM, SSD_GROUPS, SSD_STATE, SSD_CONV = 8, 64, 2, 64, 4
SSD_INNER = SSD_HEADS * SSD_HEAD_DIM
SSD_CONV_DIM = SSD_INNER + 2 * SSD_GROUPS * SSD_STATE
HG_HEADS, HG_DIM = 4, 128
FOX_HEADS, FOX_DIM = 4, 128
FFN_HIDDEN = 2816

AB_MAIN = 2816
AB_GATES = SSD_HEADS
CD_MAIN = 3584
CD_GATES = FOX_HEADS
GATE_PAD = 128

V7X_LANES = 128
V7X_VMEM_LIMIT = 56 * 1024 * 1024

ROW_TILE = 512
COL_TILE = 512
FFN_TILE = 256
CHUNK = 128
SUB = 16
HG_CHUNKS_PER_STEP = 4
ATT_ROWS = 1024
ATT_SUB = 256
ATT_KEYS = 1024


def _dot(a, b):
    return jnp.dot(a, b, preferred_element_type=F32)


def _dot_nt(a, b):
    return lax.dot_general(a, b, (((1,), (1,)), ((), ())), preferred_element_type=F32)


def _dot_tn(a, b):
    return _dot(a.T.astype(BF16), b)


def _split3(x):
    hi = x.astype(BF16)
    r = x - hi.astype(F32)
    mid = r.astype(BF16)
    lo = (r - mid.astype(F32)).astype(BF16)
    return hi, mid, lo


def _tri_dot(tri, x):
    hi, mid, lo = _split3(x)
    return _dot(tri, hi) + _dot(tri, mid) + _dot(tri, lo)


def _dot_tri(x, tri):
    hi, mid, lo = _split3(x)
    return _dot(hi, tri) + _dot(mid, tri) + _dot(lo, tri)


def _sigmoid(x):
    return 1.0 / (1.0 + jnp.exp(-x))


def _softplus(x):
    return jnp.maximum(x, 0.0) + jnp.log1p(jnp.exp(-jnp.abs(x)))


def _rms(x, w):
    return x * lax.rsqrt(jnp.mean(x * x, axis=-1, keepdims=True) + EPS) * w


def _const_spec(shape):
    zeros = (0,) * len(shape)
    return pl.BlockSpec(shape, lambda *_: zeros, pipeline_mode=pl.Buffered(1))


def _params(semantics):
    return pltpu.CompilerParams(dimension_semantics=semantics,
                                vmem_limit_bytes=V7X_VMEM_LIMIT)


def _inproj_kernel(h_ref, nw_ref, w_ref, wqk_ref, qkw_ref, out_ref, gate_ref, wb_ref, *,
                   n_main, norm_cols):
    n_all = w_ref.shape[2]

    @pl.when(pl.program_id(0) == 0)
    def _():
        for c0 in range(0, n_main, COL_TILE):
            width = min(COL_TILE, n_main - c0)
            wb_ref[:, c0:c0 + width] = w_ref[0, :, c0:c0 + width].astype(BF16)
        wb_ref[:, n_main:n_main + GATE_PAD] = jnp.zeros((D_MODEL, GATE_PAD), BF16)
        wb_ref[:, n_main:n_all] = w_ref[0, :, n_main:n_all].astype(BF16)
        if wqk_ref is not None:
            wb_ref[:, 0:wqk_ref.shape[1]] = wqk_ref[...].astype(BF16)

    x = h_ref[...]
    ub = _rms(x, nw_ref[...]).astype(BF16)
    for c0 in range(0, n_main, COL_TILE):
        width = min(COL_TILE, n_main - c0)
        y = _dot(ub, wb_ref[:, c0:c0 + width])
        for l0 in range(0, width, V7X_LANES):
            col = c0 + l0
            piece = y[:, l0:l0 + V7X_LANES]
            if norm_cols[0] <= col < norm_cols[1]:
                piece = _rms(piece, qkw_ref[:, col - norm_cols[0]:col - norm_cols[0] + V7X_LANES])
            out_ref[:, col:col + V7X_LANES] = piece.astype(out_ref.dtype)
    gate_ref[...] = _dot(ub, wb_ref[:, n_main:n_main + GATE_PAD])


def _inproj(h2d, norm_w, w_all, layer, n_main, w_qk=None, qk_w=None, norm_cols=(0, 0)):
    n_rows = h2d.shape[0]
    n_all = w_all.shape[2]

    def body(h_ref, nw_ref, w_ref, *rest):
        rest = list(rest)
        wqk_ref = rest.pop(0) if w_qk is not None else None
        qkw_ref = rest.pop(0) if qk_w is not None else None
        _inproj_kernel(h_ref, nw_ref, w_ref, wqk_ref, qkw_ref, *rest,
                       n_main=n_main, norm_cols=norm_cols)

    operands = [h2d, norm_w, w_all]
    in_specs = [
        pl.BlockSpec((ROW_TILE, D_MODEL), lambda i: (i, 0)),
        _const_spec((1, D_MODEL)),
        pl.BlockSpec((1, D_MODEL, n_all), lambda i: (layer, 0, 0), pipeline_mode=pl.Buffered(1)),
    ]
    for extra in (w_qk, qk_w):
        if extra is not None:
            operands.append(extra)
            in_specs.append(_const_spec(extra.shape))
    return pl.pallas_call(
        body,
        grid=(n_rows // ROW_TILE,),
        in_specs=in_specs,
        out_specs=[
            pl.BlockSpec((ROW_TILE, n_main), lambda i: (i, 0)),
            pl.BlockSpec((ROW_TILE, GATE_PAD), lambda i: (i, 0)),
        ],
        out_shape=[
            jax.ShapeDtypeStruct((n_rows, n_main), BF16),
            jax.ShapeDtypeStruct((n_rows, GATE_PAD), F32),
        ],
        scratch_shapes=[pltpu.VMEM((D_MODEL, n_main + GATE_PAD), BF16)],
        compiler_params=_params(("arbitrary",)),
        name="inproj",
    )(*operands)


def _ffn_kernel(ya_ref, yb_ref, h_ref, woa_ref, wob_ref, nw_ref, wg_ref, wu_ref, wd_ref,
                out_ref, act_ref):
    h1 = h_ref[...] + _dot(ya_ref[...], woa_ref[0]) + _dot(yb_ref[...], wob_ref[0])
    ub = _rms(h1, nw_ref[...]).astype(BF16)
    for c0 in range(0, FFN_HIDDEN, FFN_TILE):
        g = _dot(ub, wg_ref[0, :, c0:c0 + FFN_TILE])
        up = _dot(ub, wu_ref[0, :, c0:c0 + FFN_TILE])
        act_ref[:, c0:c0 + FFN_TILE] = (g * _sigmoid(g) * up).astype(BF16)
    out_ref[...] = h1 + _dot(act_ref[...], wd_ref[0])


def _layer_spec(shape, layer, row_block=0):
    return pl.BlockSpec((1,) + shape, lambda i: (layer, row_block, 0), pipeline_mode=pl.Buffered(1))


def _outproj_ffn(ya, yb, h2d, w_out, j, norm_w, w_gate, w_up, w_down, layer):
    n_rows = h2d.shape[0]
    half = ya.shape[1]
    return pl.pallas_call(
        _ffn_kernel,
        grid=(n_rows // ROW_TILE,),
        in_specs=[
            pl.BlockSpec((ROW_TILE, half), lambda i: (i, 0)),
            pl.BlockSpec((ROW_TILE, half), lambda i: (i, 0)),
            pl.BlockSpec((ROW_TILE, D_MODEL), lambda i: (i, 0)),
            _layer_spec((half, D_MODEL), j, 0),
            _layer_spec((half, D_MODEL), j, 1),
            _const_spec((1, D_MODEL)),
            _layer_spec((D_MODEL, FFN_HIDDEN), layer),
            _layer_spec((D_MODEL, FFN_HIDDEN), layer),
            _layer_spec((FFN_HIDDEN, D_MODEL), layer),
        ],
        out_specs=pl.BlockSpec((ROW_TILE, D_MODEL), lambda i: (i, 0)),
        out_shape=jax.ShapeDtypeStruct((n_rows, D_MODEL), F32),
        scratch_shapes=[pltpu.VMEM((ROW_TILE, FFN_HIDDEN), BF16)],
        compiler_params=_params(("parallel",)),
        name="outproj_ffn",
    )(ya, yb, h2d, w_out, w_out, norm_w, w_gate, w_up, w_down)


def _even_mixer_kernel(proj_ref, gate_ref, gate_t_ref, cos_ref, sin_ref, gnw_ref,
                       convw_ref, convb_ref, dtb_ref, dtb_t_ref, alog_ref, alog_t_ref,
                       dskip_ref, normw_ref,
                       yret_ref, yssd_ref,
                       ret_state, ssd_state, conv_buf, xbc_buf, decay_tab, qdec_tab, kdec_tab,
                       alast_tab):
    c = CHUNK
    t = pl.program_id(1)

    row_i = lax.broadcasted_iota(jnp.int32, (c, c), 0)
    col_j = lax.broadcasted_iota(jnp.int32, (c, c), 1)
    causal = row_i >= col_j
    lane = lax.broadcasted_iota(jnp.int32, (1, V7X_LANES), 1)
    log_gammas = [math.log1p(-(2.0 ** (-5 - h))) for h in range(RET_HEADS)]

    @pl.when(t == 0)
    def _():
        ret_state[...] = jnp.zeros_like(ret_state)
        ssd_state[...] = jnp.zeros_like(ssd_state)
        conv_buf[0:8, :] = jnp.zeros((8, SSD_CONV_DIM), F32)
        dij = (row_i - col_j).astype(F32)
        pos = lax.broadcasted_iota(jnp.int32, (c, V7X_LANES), 0).astype(F32)
        for h, log_g in enumerate(log_gammas):
            decay_tab[h] = jnp.exp(jnp.where(causal, dij * log_g, -jnp.inf))
            qdec_tab[h] = jnp.exp((pos + 1.0) * log_g)
        lane_log_g = jnp.zeros((1, V7X_LANES), F32)
        row_head = lax.broadcasted_iota(jnp.int32, (V7X_LANES, V7X_LANES), 0) // (RET_DK // 2)
        row_log_g = jnp.zeros((V7X_LANES, V7X_LANES), F32)
        for h, log_g in enumerate(log_gammas):
            lane_log_g = jnp.where(lane // (RET_DK // 2) == h, log_g, lane_log_g)
            row_log_g = jnp.where(row_head == h, log_g, row_log_g)
        kdec_tab[...] = jnp.exp((float(c - 1) - pos) * lane_log_g) * (RET_DK ** -0.5)
        alast_tab[...] = jnp.exp(float(c) * row_log_g)

    cos = cos_ref[...]
    sin = sin_ref[...]
    q1 = proj_ref[0, :, 0:128].astype(F32)
    q2 = proj_ref[0, :, 128:256].astype(F32)
    k1 = proj_ref[0, :, 256:384].astype(F32)
    k2 = proj_ref[0, :, 384:512].astype(F32)
    rq1 = q1 * cos - q2 * sin
    rq2 = q1 * sin + q2 * cos
    k_scale = RET_DK ** -0.5
    rk1 = k1 * cos - k2 * sin
    rk2 = k1 * sin + k2 * cos
    k_t = (jnp.concatenate([rk1.T, rk2.T], axis=0) * k_scale).astype(BF16)
    lane_head = lane // (RET_DK // 2)
    for h in range(RET_HEADS):
        hm = lane_head == h
        qm = jnp.concatenate([jnp.where(hm, rq1, 0.0), jnp.where(hm, rq2, 0.0)],
                             axis=-1).astype(BF16)
        p = (_dot(qm, k_t) * decay_tab[h]).astype(BF16)
        v_h = proj_ref[0, :, 512 + 128 * h:640 + 128 * h]
        s_h = ret_state[:, 128 * h:128 * h + 128].astype(BF16)
        o = _dot(p, v_h) + _dot(qm, s_h) * qdec_tab[h]
        mu = jnp.mean(o, axis=-1, keepdims=True)
        d = o - mu
        var = jnp.mean(d * d, axis=-1, keepdims=True)
        on = d * lax.rsqrt(var + EPS) * gnw_ref[:, 128 * h:128 * h + 128]
        g = proj_ref[0, :, 1024 + 128 * h:1152 + 128 * h].astype(F32)
        yret_ref[0, :, 128 * h:128 * h + 128] = (on * (g * _sigmoid(g))).astype(yret_ref.dtype)
    kdec = kdec_tab[...]
    v_all = proj_ref[0, :, 512:1024]
    upd = jnp.concatenate([_dot_tn(rk1 * kdec, v_all), _dot_tn(rk2 * kdec, v_all)], axis=0)
    alast = alast_tab[...]
    for r0 in range(0, 2 * V7X_LANES, V7X_LANES):
        for c0 in range(0, RET_HEADS * RET_DV, V7X_LANES):
            ret_state[r0:r0 + V7X_LANES, c0:c0 + V7X_LANES] = (
                ret_state[r0:r0 + V7X_LANES, c0:c0 + V7X_LANES] * alast
                + upd[r0:r0 + V7X_LANES, c0:c0 + V7X_LANES])

    conv_buf[8:8 + c, :] = proj_ref[0, :, 2048:2816].astype(F32)
    acc = jnp.zeros((c, SSD_CONV_DIM), F32) + convb_ref[...]
    for k in range(SSD_CONV):
        off = 8 - (SSD_CONV - 1) + k
        acc = acc + convw_ref[k:k + 1, :] * conv_buf[off:off + c, :]
    conv_buf[0:8, :] = conv_buf[c:c + 8, :]
    xbc_buf[...] = acc * _sigmoid(acc)
    xs = xbc_buf[:, 0:SSD_INNER]
    bm = xbc_buf[:, 512:640]
    cm = xbc_buf[:, 640:768]

    dt = _softplus(gate_ref[0] + dtb_ref[...])
    la = -dt * jnp.exp(alog_ref[...])
    dt_t = _softplus(gate_t_ref[0] + dtb_t_ref[...])
    la_t = -dt_t * jnp.exp(alog_t_ref[...])
    tri = jnp.where(causal, 1.0, 0.0).astype(BF16)
    tri_t = jnp.where(row_i <= col_j, 1.0, 0.0).astype(BF16)
    cum = _tri_dot(tri, la)
    cum_t = _dot_tri(la_t, tri_t)
    cum_last = cum[c - 1:c, :]
    q_scale = jnp.exp(cum)
    k_scale_ssd = dt * jnp.exp(cum_last - cum)
    s_scale = jnp.exp(cum_last)
    lane_half = lane // SSD_HEAD_DIM
    y_pairs = []
    for g in range(SSD_GROUPS):
        gm = lane_half == g
        cg = jnp.where(gm, cm, 0.0)
        bg = jnp.where(gm, bm, 0.0)
        cg_b = cg.astype(BF16)
        gmat = _dot_nt(cg_b, bg.astype(BF16))
        heads_per_group = SSD_HEADS // SSD_GROUPS
        for hh in range(heads_per_group):
            h = g * heads_per_group + hh
            pair, half = h // 2, h % 2
            x_pair = xs[:, 128 * pair:128 * pair + 128]
            xm = jnp.where(lane_half == half, x_pair, 0.0).astype(BF16)
            cc = cum[:, h:h + 1]
            cr = cum_t[h:h + 1, :]
            lmat = jnp.exp(jnp.where(causal, cc - cr, -jnp.inf)) * dt_t[h:h + 1, :]
            p = (gmat * lmat).astype(BF16)
            s_h = ssd_state[h]
            y = _dot(p, xm) + _dot((cg * q_scale[:, h:h + 1]).astype(BF16), s_h.astype(BF16))
            ssd_state[h] = s_h * s_scale[:, h:h + 1] + _dot_tn(bg * k_scale_ssd[:, h:h + 1], xm)
            if half == 0:
                y_pairs.append(y)
            else:
                y_pairs[pair] = y_pairs[pair] + y
    y = jnp.concatenate(y_pairs, axis=-1) + dskip_ref[...] * xs
    z = proj_ref[0, :, 1536:2048].astype(F32)
    y = y * (z * _sigmoid(z))
    group = SSD_INNER // SSD_GROUPS
    for g in range(SSD_GROUPS):
        yssd_ref[0, :, group * g:group * (g + 1)] = _rms(
            y[:, group * g:group * (g + 1)],
            normw_ref[:, group * g:group * (g + 1)]).astype(yssd_ref.dtype)


def _even_mixer(proj, gates, gates_t, cos, sin, gn_w, conv_w, conv_b, dt_b, dt_b_t,
                a_log, a_log_t, d_skip, norm_w):
    b, t_len, _ = proj.shape
    c = CHUNK
    half = RET_HEADS * RET_DV
    return pl.pallas_call(
        _even_mixer_kernel,
        grid=(b, t_len // c),
        in_specs=[
            pl.BlockSpec((1, c, AB_MAIN), lambda i, j: (i, j, 0)),
            pl.BlockSpec((1, c, GATE_PAD), lambda i, j: (i, j, 0)),
            pl.BlockSpec((1, 8, c), lambda i, j: (i, 0, j)),
            pl.BlockSpec((c, V7X_LANES), lambda i, j: (j, 0)),
            pl.BlockSpec((c, V7X_LANES), lambda i, j: (j, 0)),
            _const_spec(gn_w.shape), _const_spec(conv_w.shape), _const_spec(conv_b.shape),
            _const_spec(dt_b.shape), _const_spec(dt_b_t.shape),
            _const_spec(a_log.shape), _const_spec(a_log_t.shape),
            _const_spec(d_skip.shape), _const_spec(norm_w.shape),
        ],
        out_specs=[
            pl.BlockSpec((1, c, half), lambda i, j: (i, j, 0)),
            pl.BlockSpec((1, c, SSD_INNER), lambda i, j: (i, j, 0)),
        ],
        out_shape=[
            jax.ShapeDtypeStruct((b, t_len, half), BF16),
            jax.ShapeDtypeStruct((b, t_len, SSD_INNER), BF16),
        ],
        scratch_shapes=[
            pltpu.VMEM((2 * V7X_LANES, RET_HEADS * RET_DV), F32),
            pltpu.VMEM((SSD_HEADS, V7X_LANES, V7X_LANES), F32),
            pltpu.VMEM((c + 8, SSD_CONV_DIM), F32),
            pltpu.VMEM((c, SSD_CONV_DIM), F32),
            pltpu.VMEM((RET_HEADS, c, c), F32),
            pltpu.VMEM((RET_HEADS, c, V7X_LANES), F32),
            pltpu.VMEM((c, V7X_LANES), F32),
            pltpu.VMEM((V7X_LANES, V7X_LANES), F32),
        ],
        compiler_params=_params(("parallel", "arbitrary")),
        name="even_mixer",
    )(proj, gates, gates_t, cos, sin, gn_w, conv_w, conv_b, dt_b, dt_b_t,
      a_log, a_log_t, d_skip, norm_w)


def _hgrn2_kernel(proj_ref, lbl_ref, normw_ref, y_ref, state_t, *, layer):
    c = CHUNK
    t = pl.program_id(1)

    @pl.when(t == 0)
    def _():
        state_t[...] = jnp.zeros_like(state_t)

    logits = lbl_ref[...]
    e = jnp.exp(logits - jnp.max(logits, axis=0, keepdims=True))
    prob = e / jnp.sum(e, axis=0, keepdims=True)
    lb = jnp.zeros((1, HG_HEADS * HG_DIM), F32)
    for i in range(1, layer + 1):
        lb = lb + prob[i:i + 1, :]

    row_i = lax.broadcasted_iota(jnp.int32, (c, c), 0)
    col_j = lax.broadcasted_iota(jnp.int32, (c, c), 1)
    tri = jnp.where(row_i >= col_j, 1.0, 0.0).astype(BF16)

    width = HG_HEADS * HG_DIM

    def chunk(r0, states):
        rows = slice(r0, r0 + c)
        zf = proj_ref[0, rows, width:2 * width].astype(F32)
        f = lb + (1.0 - lb) * _sigmoid(zf)
        k_all = (1.0 - lb) * _sigmoid(-zf)
        cum_all = _tri_dot(tri, jnp.log(f))
        new_states = []
        for h in range(HG_HEADS):
            sl = slice(HG_DIM * h, HG_DIM * (h + 1))
            q = proj_ref[0, rows, HG_DIM * h:HG_DIM * (h + 1)].astype(F32)
            k = k_all[:, sl]
            cum = cum_all[:, sl]
            v = proj_ref[0, rows, 2 * width + HG_DIM * h:2 * width + HG_DIM * (h + 1)]
            st = states[h]
            inter = _dot_nt((q * jnp.exp(cum)).astype(BF16), st.astype(BF16))
            n_blk = c // SUB
            ends = [cum[(j + 1) * SUB - 1:(j + 1) * SUB, :] for j in range(n_blk)]
            starts = [jnp.zeros((1, HG_DIM), F32)] + ends[:-1]
            start_rows = jnp.concatenate(
                [jnp.broadcast_to(s, (SUB, HG_DIM)) for s in starts], axis=0)
            end_rows = jnp.concatenate(
                [jnp.broadcast_to(e, (SUB, HG_DIM)) for e in ends], axis=0)
            q_blk = (q * jnp.exp(cum - start_rows)).astype(BF16)
            k_end = k * jnp.exp(end_rows - cum)
            k_end_b = k_end.astype(BF16)
            k_own = (k * jnp.exp(start_rows - cum)).astype(BF16)
            blocks = []
            for i in range(n_blk):
                pieces = []
                for j in range(i):
                    if j == i - 1:
                        pieces.append(k_end_b[j * SUB:(j + 1) * SUB, :])
                    else:
                        gap = jnp.exp(starts[i] - ends[j])
                        pieces.append((k_end[j * SUB:(j + 1) * SUB, :] * gap).astype(BF16))
                pieces.append(k_own[i * SUB:(i + 1) * SUB, :])
                if i + 1 < n_blk:
                    pieces.append(jnp.zeros((c - (i + 1) * SUB, HG_DIM), BF16))
                kb = jnp.concatenate(pieces, axis=0)
                blocks.append(_dot_nt(q_blk[i * SUB:(i + 1) * SUB, :], kb))
            scores = jnp.where(row_i >= col_j, jnp.concatenate(blocks, axis=0), 0.0)
            o = inter + _dot(scores.astype(BF16), v)
            cl = cum[c - 1:c, :]
            k_out = (k * jnp.exp(cl - cum)).astype(BF16)
            new_states.append(st * jnp.exp(cl) + _dot_tn(v.astype(F32), k_out))
            g = proj_ref[0, rows, 3 * width + HG_DIM * h:3 * width + HG_DIM * (h + 1)].astype(F32)
            y_ref[0, rows, sl] = (_rms(o, normw_ref[...]) * (g * _sigmoid(g))).astype(y_ref.dtype)
        return new_states

    states = [state_t[h] for h in range(HG_HEADS)]
    for ci in range(HG_CHUNKS_PER_STEP):
        states = chunk(ci * c, states)
    for h in range(HG_HEADS):
        state_t[h] = states[h]


def _hgrn2(proj, lb_logits, norm_w, layer):
    b, t_len, _ = proj.shape
    c = CHUNK * HG_CHUNKS_PER_STEP
    width = HG_HEADS * HG_DIM
    kern = functools.partial(_hgrn2_kernel, layer=layer)
    return pl.pallas_call(
        kern,
        grid=(b, t_len // c),
        in_specs=[
            pl.BlockSpec((1, c, 4 * width), lambda i, j: (i, j, 0)),
            _const_spec(lb_logits.shape),
            _const_spec(norm_w.shape),
        ],
        out_specs=pl.BlockSpec((1, c, width), lambda i, j: (i, j, 0)),
        out_shape=jax.ShapeDtypeStruct((b, t_len, width), BF16),
        scratch_shapes=[pltpu.VMEM((HG_HEADS, HG_DIM, HG_DIM), F32)],
        compiler_params=_params(("parallel", "arbitrary")),
        name="hgrn2",
    )(proj, lb_logits, norm_w)


def _forget_bias_kernel(gate_ref, bias_ref, out_ref):
    t_len = gate_ref.shape[1]
    blk = V7X_LANES
    row_i = lax.broadcasted_iota(jnp.int32, (blk, blk), 0)
    col_j = lax.broadcasted_iota(jnp.int32, (blk, blk), 1)
    tri = jnp.where(row_i >= col_j, 1.0, 0.0).astype(BF16)
    lane = lax.broadcasted_iota(jnp.int32, (1, V7X_LANES), 1)
    offset = jnp.zeros((1, V7X_LANES), F32)
    for i in range(t_len // blk):
        x = gate_ref[0, i * blk:(i + 1) * blk, :] + bias_ref[...]
        log_f = jnp.minimum(x, 0.0) - jnp.log1p(jnp.exp(-jnp.abs(x)))
        local = _tri_dot(tri, log_f)
        hi, mid, lo = _split3((local + offset) * -LOG2_E)
        offset = offset + local[blk - 1:blk, :]
        hi, mid, lo = hi.astype(F32), mid.astype(F32), lo.astype(F32)
        for h in range(FOX_HEADS):
            parts = jnp.where(lane == 0, hi[:, h:h + 1],
                              jnp.where(lane == 1, mid[:, h:h + 1],
                                        jnp.where(lane == 2, lo[:, h:h + 1], 0.0)))
            out_ref[0, i * blk:(i + 1) * blk, h * V7X_LANES:(h + 1) * V7X_LANES] = parts.astype(BF16)


def _forget_bias(gates, bias):
    b, t_len, _ = gates.shape
    return pl.pallas_call(
        _forget_bias_kernel,
        grid=(b,),
        in_specs=[pl.BlockSpec((1, t_len, GATE_PAD), lambda i: (i, 0, 0)), _const_spec(bias.shape)],
        out_specs=pl.BlockSpec((1, t_len, FOX_HEADS * V7X_LANES), lambda i: (i, 0, 0)),
        out_shape=jax.ShapeDtypeStruct((b, t_len, FOX_HEADS * V7X_LANES), BF16),
        compiler_params=_params(("parallel",)),
        name="forget_bias",
    )(gates, bias)


def _fox_kernel(q_ref, k_ref, kb_ref, v_ref, y_ref):
    sub, keys = ATT_SUB, ATT_KEYS
    n_sub = ATT_ROWS // sub
    qi = pl.program_id(2)
    q0 = pl.multiple_of(qi * ATT_ROWS, ATT_ROWS)
    lane = lax.broadcasted_iota(jnp.int32, (sub, V7X_LANES), 1)
    q_ones = jnp.where(lane < 3, 1.0, 0.0).astype(BF16)

    def tile(state, r, k, v, mask):
        m, acc = state
        q = jnp.concatenate([q_ref[0, r * sub:(r + 1) * sub, :], q_ones], axis=-1)
        s = _dot_nt(q, k)
        if mask is not None:
            s = jnp.where(mask, s, -jnp.inf)
        m_new = jnp.maximum(m, jnp.max(s, axis=-1, keepdims=True))
        alpha = jnp.exp2(m - m_new)
        p = jnp.exp2(s - m_new)
        acc = alpha * acc + _dot(p.astype(BF16), v)
        return m_new, acc

    def load(k0, width):
        k = jnp.concatenate([k_ref[0, pl.ds(k0, width), :], kb_ref[0, pl.ds(k0, width), :]], axis=-1)
        v = jnp.concatenate([v_ref[0, pl.ds(k0, width), :], jnp.ones((width, V7X_LANES), BF16)],
                            axis=-1)
        return k, v

    def before(kj, states):
        k, v = load(pl.multiple_of(kj * keys, keys), keys)
        return tuple(tile(states[r], r, k, v, None) for r in range(n_sub))

    init = tuple((jnp.full((sub, 1), -jnp.inf, F32), jnp.zeros((sub, 2 * FOX_DIM), F32))
                 for _ in range(n_sub))
    states = list(lax.fori_loop(0, qi * (ATT_ROWS // keys), before, init))

    row_i = lax.broadcasted_iota(jnp.int32, (sub, sub), 0)
    col_j = lax.broadcasted_iota(jnp.int32, (sub, sub), 1)
    causal = row_i >= col_j
    for c in range(n_sub):
        k, v = load(pl.multiple_of(q0 + c * sub, sub), sub)
        for r in range(c, n_sub):
            states[r] = tile(states[r], r, k, v, causal if r == c else None)
    for r in range(n_sub):
        _, acc = states[r]
        y_ref[0, r * sub:(r + 1) * sub, :] = (acc[:, 0:FOX_DIM] / acc[:, FOX_DIM:]).astype(y_ref.dtype)


def _fox(proj, key_bias):
    b, t_len, _ = proj.shape
    blk = ATT_ROWS
    qcol, kcol, vcol = (2048 // FOX_DIM, 2560 // FOX_DIM, 3072 // FOX_DIM)
    return pl.pallas_call(
        _fox_kernel,
        grid=(b, FOX_HEADS, t_len // blk),
        in_specs=[
            pl.BlockSpec((1, blk, FOX_DIM), lambda i, h, j: (i, j, qcol + h)),
            pl.BlockSpec((1, t_len, FOX_DIM), lambda i, h, j: (i, 0, kcol + h)),
            pl.BlockSpec((1, t_len, V7X_LANES), lambda i, h, j: (i, 0, h)),
            pl.BlockSpec((1, t_len, FOX_DIM), lambda i, h, j: (i, 0, vcol + h)),
        ],
        out_specs=pl.BlockSpec((1, blk, FOX_DIM), lambda i, h, j: (i, j, h)),
        out_shape=jax.ShapeDtypeStruct((b, t_len, FOX_HEADS * FOX_DIM), BF16),
        compiler_params=_params(("parallel", "parallel", "arbitrary")),
        name="fox_attention",
    )(proj, proj, key_bias, proj)


def _rotary_perm():
    width = RET_HEADS * RET_DK
    perm = np.zeros((2 * width,), np.int32)
    for base in (0, width):
        for half in range(2):
            for h in range(RET_HEADS):
                for i in range(RET_DK // 2):
                    perm[base + half * 128 + h * 32 + i] = base + h * RET_DK + 2 * i + half
    return perm


def _pad_lanes(a, width=GATE_PAD):
    return jnp.pad(a, [(0, 0)] * (a.ndim - 1) + [(0, width - a.shape[-1])])


def kernel(x, norm_mix, norm_ffn, ffn_w_gate, ffn_w_up, ffn_w_down, ab_w_in, ab_w_out, ret_gn_w,
           ssd_conv_w, ssd_conv_b, ssd_dt_bias, ssd_a_log, ssd_d, ssd_norm_w, cd_w_in, cd_w_out,
           hg_lb_logits, hg_norm_w, fox_f_bias, fox_q_norm_w, fox_k_norm_w):
    b, t_len, d = x.shape
    depth = norm_mix.shape[0]
    n_rows = b * t_len
    perm = _rotary_perm()

    half = RET_DK // 2
    freqs = ROPE_BASE ** (-jnp.linspace(0.0, 1.0, half, dtype=F32))
    ang = jnp.arange(t_len, dtype=F32)[:, None] * freqs[None, :]
    cos = jnp.tile(jnp.cos(ang), (1, RET_HEADS))
    sin = jnp.tile(jnp.sin(ang), (1, RET_HEADS))

    ab_w_out_b, cd_w_out_b = ab_w_out.astype(BF16), cd_w_out.astype(BF16)
    w_gate_b, w_up_b, w_down_b = (ffn_w_gate.astype(BF16), ffn_w_up.astype(BF16),
                                  ffn_w_down.astype(BF16))

    h2d = x.reshape(n_rows, d)
    for layer in range(depth):
        j = layer // 2
        if layer % 2 == 0:
            w_qk = ab_w_in[j, :, 0:2 * RET_HEADS * RET_DK][:, perm]
            proj, gates = _inproj(h2d, norm_mix[layer][None, :], ab_w_in, j, AB_MAIN, w_qk=w_qk)
            gates3 = gates.reshape(b, t_len, GATE_PAD)
            gates_t = jnp.transpose(gates3[:, :, 0:8], (0, 2, 1))
            y_a, y_b = _even_mixer(
                proj.reshape(b, t_len, AB_MAIN), gates3, gates_t, cos, sin,
                ret_gn_w[j].reshape(1, RET_HEADS * RET_DV),
                ssd_conv_w[j], ssd_conv_b[j][None, :],
                _pad_lanes(ssd_dt_bias[j][None, :]), ssd_dt_bias[j][:, None],
                _pad_lanes(ssd_a_log[j][None, :]), ssd_a_log[j][:, None],
                jnp.repeat(ssd_d[j], SSD_HEAD_DIM)[None, :], ssd_norm_w[j][None, :])
            w_out = ab_w_out_b
        else:
            qk_w = jnp.concatenate([jnp.tile(fox_q_norm_w[j] * (FOX_DIM ** -0.5 * LOG2_E), FOX_HEADS),
                                    jnp.tile(fox_k_norm_w[j], FOX_HEADS)])[None, :]
            proj, gates = _inproj(h2d, norm_mix[layer][None, :], cd_w_in, j, CD_MAIN,
                                  qk_w=qk_w, norm_cols=(2048, 3072))
            proj3 = proj.reshape(b, t_len, CD_MAIN)
            key_bias = _forget_bias(gates.reshape(b, t_len, GATE_PAD),
                                    _pad_lanes(fox_f_bias[j][None, :]))
            y_a = _hgrn2(proj3, hg_lb_logits, hg_norm_w[j][None, :], j)
            y_b = _fox(proj3, key_bias)
            w_out = cd_w_out_b
        half_out = w_out.shape[1] // 2
        h2d = _outproj_ffn(
            y_a.reshape(n_rows, half_out), y_b.reshape(n_rows, half_out), h2d,
            w_out, j, norm_ffn[layer][None, :], w_gate_b, w_up_b, w_down_b, layer)
    return h2d.reshape(b, t_len, d)
```

```python
import functools
import math

import numpy as np
import jax
import jax.numpy as jnp
from jax import lax
from jax.experimental import pallas as pl
from jax.experimental.pallas import tpu as pltpu

F32 = jnp.float32
BF16 = jnp.bfloat16

D_MODEL = 1024
EPS = 1e-6
ROPE_BASE = 10000.0
LOG2_E = 1.4426950408889634

RET_HEADS, RET_DK, RET_DV = 4, 64, 128
SSD_HEADS, SSD_HEAD_DIM, SSD_GROUPS, SSD_STATE, SSD_CONV = 8, 64, 2, 64, 4
SSD_INNER = SSD_HEADS * SSD_HEAD_DIM
SSD_CONV_DIM = SSD_INNER + 2 * SSD_GROUPS * SSD_STATE
HG_HEADS, HG_DIM = 4, 128
FOX_HEADS, FOX_DIM = 4, 128
FFN_HIDDEN = 2816

AB_MAIN = 2816
AB_GATES = SSD_HEADS
CD_MAIN = 3584
CD_GATES = FOX_HEADS
GATE_PAD = 128

V7X_LANES = 128
V7X_VMEM_LIMIT = 56 * 1024 * 1024

ROW_TILE = 512
COL_TILE = 512
FFN_TILE = 256
CHUNK = 128
SUB = 16
ATT_ROWS = 1024
ATT_SUB = 256
ATT_KEYS = 1024


def _dot(a, b):
    return jnp.dot(a, b, preferred_element_type=F32)


def _dot_nt(a, b):
    return lax.dot_general(a, b, (((1,), (1,)), ((), ())), preferred_element_type=F32)


def _dot_tn(a, b):
    return _dot(a.T.astype(BF16), b)


def _split3(x):
    hi = x.astype(BF16)
    r = x - hi.astype(F32)
    mid = r.astype(BF16)
    lo = (r - mid.astype(F32)).astype(BF16)
    return hi, mid, lo


def _tri_dot(tri, x):
    hi, mid, lo = _split3(x)
    return _dot(tri, hi) + _dot(tri, mid) + _dot(tri, lo)


def _dot_tri(x, tri):
    hi, mid, lo = _split3(x)
    return _dot(hi, tri) + _dot(mid, tri) + _dot(lo, tri)


def _sigmoid(x):
    return 1.0 / (1.0 + jnp.exp(-x))


def _softplus(x):
    return jnp.maximum(x, 0.0) + jnp.log1p(jnp.exp(-jnp.abs(x)))


def _rms(x, w):
    return x * lax.rsqrt(jnp.mean(x * x, axis=-1, keepdims=True) + EPS) * w


def _const_spec(shape):
    zeros = (0,) * len(shape)
    return pl.BlockSpec(shape, lambda *_: zeros, pipeline_mode=pl.Buffered(1))


def _params(semantics):
    return pltpu.CompilerParams(dimension_semantics=semantics,
                                vmem_limit_bytes=V7X_VMEM_LIMIT)


def _inproj_kernel(h_ref, nw_ref, w_ref, wqk_ref, qkw_ref, out_ref, gate_ref, wb_ref, *,
                   n_main, norm_cols):
    n_all = w_ref.shape[2]

    @pl.when(pl.program_id(0) == 0)
    def _():
        for c0 in range(0, n_main, COL_TILE):
            width = min(COL_TILE, n_main - c0)
            wb_ref[:, c0:c0 + width] = w_ref[0, :, c0:c0 + width].astype(BF16)
        wb_ref[:, n_main:n_main + GATE_PAD] = jnp.zeros((D_MODEL, GATE_PAD), BF16)
        wb_ref[:, n_main:n_all] = w_ref[0, :, n_main:n_all].astype(BF16)
        if wqk_ref is not None:
            wb_ref[:, 0:wqk_ref.shape[1]] = wqk_ref[...].astype(BF16)

    x = h_ref[...]
    ub = _rms(x, nw_ref[...]).astype(BF16)
    for c0 in range(0, n_main, COL_TILE):
        width = min(COL_TILE, n_main - c0)
        y = _dot(ub, wb_ref[:, c0:c0 + width])
        for l0 in range(0, width, V7X_LANES):
            col = c0 + l0
            piece = y[:, l0:l0 + V7X_LANES]
            if norm_cols[0] <= col < norm_cols[1]:
                piece = _rms(piece, qkw_ref[:, col - norm_cols[0]:col - norm_cols[0] + V7X_LANES])
            out_ref[:, col:col + V7X_LANES] = piece.astype(out_ref.dtype)
    gate_ref[...] = _dot(ub, wb_ref[:, n_main:n_main + GATE_PAD])


def _inproj(h2d, norm_w, w_all, layer, n_main, w_qk=None, qk_w=None, norm_cols=(0, 0)):
    n_rows = h2d.shape[0]
    n_all = w_all.shape[2]

    def body(h_ref, nw_ref, w_ref, *rest):
        rest = list(rest)
        wqk_ref = rest.pop(0) if w_qk is not None else None
        qkw_ref = rest.pop(0) if qk_w is not None else None
        _inproj_kernel(h_ref, nw_ref, w_ref, wqk_ref, qkw_ref, *rest,
                       n_main=n_main, norm_cols=norm_cols)

    operands = [h2d, norm_w, w_all]
    in_specs = [
        pl.BlockSpec((ROW_TILE, D_MODEL), lambda i: (i, 0)),
        _const_spec((1, D_MODEL)),
        pl.BlockSpec((1, D_MODEL, n_all), lambda i: (layer, 0, 0), pipeline_mode=pl.Buffered(1)),
    ]
    for extra in (w_qk, qk_w):
        if extra is not None:
            operands.append(extra)
            in_specs.append(_const_spec(extra.shape))
    return pl.pallas_call(
        body,
        grid=(n_rows // ROW_TILE,),
        in_specs=in_specs,
        out_specs=[
            pl.BlockSpec((ROW_TILE, n_main), lambda i: (i, 0)),
            pl.BlockSpec((ROW_TILE, GATE_PAD), lambda i: (i, 0)),
        ],
        out_shape=[
            jax.ShapeDtypeStruct((n_rows, n_main), BF16),
            jax.ShapeDtypeStruct((n_rows, GATE_PAD), F32),
        ],
        scratch_shapes=[pltpu.VMEM((D_MODEL, n_main + GATE_PAD), BF16)],
        compiler_params=_params(("arbitrary",)),
        name="inproj",
    )(*operands)


def _ffn_tile(ya, yb, h, woa_ref, wob_ref, nw_ref, wg_ref, wu_ref, wd_ref, act_ref, between):
    h1 = h + _dot(ya, woa_ref[0]) + _dot(yb, wob_ref[0])
    ub = _rms(h1, nw_ref[...]).astype(BF16)
    col_starts = list(range(0, FFN_HIDDEN, FFN_TILE))
    per_group = -(-len(col_starts) // len(between))
    for gi, run_mixer_chunk in enumerate(between):
        run_mixer_chunk()
        for c0 in col_starts[gi * per_group:(gi + 1) * per_group]:
            g = _dot(ub, wg_ref[0, :, c0:c0 + FFN_TILE])
            up = _dot(ub, wu_ref[0, :, c0:c0 + FFN_TILE])
            act_ref[:, c0:c0 + FFN_TILE] = (g * _sigmoid(g) * up).astype(BF16)
    return h1 + _dot(act_ref[...], wd_ref[0])


def _layer_spec(shape, layer, row_block=0):
    return pl.BlockSpec((1,) + shape, lambda i: (layer, row_block, 0), pipeline_mode=pl.Buffered(1))


def _even_tables(decay_tab, qdec_tab, kdec_tab, alast_tab):
    c = CHUNK
    row_i = lax.broadcasted_iota(jnp.int32, (c, c), 0)
    col_j = lax.broadcasted_iota(jnp.int32, (c, c), 1)
    lane = lax.broadcasted_iota(jnp.int32, (1, V7X_LANES), 1)
    log_gammas = [math.log1p(-(2.0 ** (-5 - h))) for h in range(RET_HEADS)]
    dij = (row_i - col_j).astype(F32)
    pos = lax.broadcasted_iota(jnp.int32, (c, V7X_LANES), 0).astype(F32)
    for h, log_g in enumerate(log_gammas):
        decay_tab[h] = jnp.exp(jnp.where(row_i >= col_j, dij * log_g, -jnp.inf))
        qdec_tab[h] = jnp.exp((pos + 1.0) * log_g)
    lane_log_g = jnp.zeros((1, V7X_LANES), F32)
    row_head = lax.broadcasted_iota(jnp.int32, (V7X_LANES, V7X_LANES), 0) // (RET_DK // 2)
    row_log_g = jnp.zeros((V7X_LANES, V7X_LANES), F32)
    for h, log_g in enumerate(log_gammas):
        lane_log_g = jnp.where(lane // (RET_DK // 2) == h, log_g, lane_log_g)
        row_log_g = jnp.where(row_head == h, log_g, row_log_g)
    kdec_tab[...] = jnp.exp((float(c - 1) - pos) * lane_log_g) * (RET_DK ** -0.5)
    alast_tab[...] = jnp.exp(float(c) * row_log_g)


def _even_chunk(r0, proj_ref, gate_ref, gate_t_ref, cos_ref, sin_ref, gnw_ref,
                convw_ref, convb_ref, dtb_ref, dtb_t_ref, alog_ref, alog_t_ref,
                dskip_ref, normw_ref, store_ret, store_ssd,
                ret_state, ssd_state, conv_buf, xbc_buf, decay_tab, qdec_tab, kdec_tab,
                alast_tab):
    c = CHUNK
    rows = slice(r0, r0 + c)
    row_i = lax.broadcasted_iota(jnp.int32, (c, c), 0)
    col_j = lax.broadcasted_iota(jnp.int32, (c, c), 1)
    causal = row_i >= col_j
    lane = lax.broadcasted_iota(jnp.int32, (1, V7X_LANES), 1)

    cos = cos_ref[rows, :]
    sin = sin_ref[rows, :]
    q1 = proj_ref[0, rows, 0:128].astype(F32)
    q2 = proj_ref[0, rows, 128:256].astype(F32)
    k1 = proj_ref[0, rows, 256:384].astype(F32)
    k2 = proj_ref[0, rows, 384:512].astype(F32)
    rq1 = q1 * cos - q2 * sin
    rq2 = q1 * sin + q2 * cos
    k_scale = RET_DK ** -0.5
    rk1 = k1 * cos - k2 * sin
    rk2 = k1 * sin + k2 * cos
    k_t = (jnp.concatenate([rk1.T, rk2.T], axis=0) * k_scale).astype(BF16)
    lane_head = lane // (RET_DK // 2)
    for h in range(RET_HEADS):
        hm = lane_head == h
        qm = jnp.concatenate([jnp.where(hm, rq1, 0.0), jnp.where(hm, rq2, 0.0)],
                             axis=-1).astype(BF16)
        p = (_dot(qm, k_t) * decay_tab[h]).astype(BF16)
        v_h = proj_ref[0, rows, 512 + 128 * h:640 + 128 * h]
        s_h = ret_state[:, 128 * h:128 * h + 128].astype(BF16)
        o = _dot(p, v_h) + _dot(qm, s_h) * qdec_tab[h]
        mu = jnp.mean(o, axis=-1, keepdims=True)
        d = o - mu
        var = jnp.mean(d * d, axis=-1, keepdims=True)
        on = d * lax.rsqrt(var + EPS) * gnw_ref[:, 128 * h:128 * h + 128]
        g = proj_ref[0, rows, 1024 + 128 * h:1152 + 128 * h].astype(F32)
        store_ret(rows, slice(128 * h, 128 * h + 128), (on * (g * _sigmoid(g))).astype(BF16))
    kdec = kdec_tab[...]
    v_all = proj_ref[0, rows, 512:1024]
    upd = jnp.concatenate([_dot_tn(rk1 * kdec, v_all), _dot_tn(rk2 * kdec, v_all)], axis=0)
    alast = alast_tab[...]
    for r0 in range(0, 2 * V7X_LANES, V7X_LANES):
        for c0 in range(0, RET_HEADS * RET_DV, V7X_LANES):
            ret_state[r0:r0 + V7X_LANES, c0:c0 + V7X_LANES] = (
                ret_state[r0:r0 + V7X_LANES, c0:c0 + V7X_LANES] * alast
                + upd[r0:r0 + V7X_LANES, c0:c0 + V7X_LANES])

    conv_buf[8:8 + c, :] = proj_ref[0, rows, 2048:2816].astype(F32)
    acc = jnp.zeros((c, SSD_CONV_DIM), F32) + convb_ref[...]
    for k in range(SSD_CONV):
        off = 8 - (SSD_CONV - 1) + k
        acc = acc + convw_ref[k:k + 1, :] * conv_buf[off:off + c, :]
    conv_buf[0:8, :] = conv_buf[c:c + 8, :]
    xbc_buf[...] = acc * _sigmoid(acc)
    xs = xbc_buf[:, 0:SSD_INNER]
    bm = xbc_buf[:, 512:640]
    cm = xbc_buf[:, 640:768]

    dt = _softplus(gate_ref[0, rows, :] + dtb_ref[...])
    la = -dt * jnp.exp(alog_ref[...])
    dt_t = _softplus(gate_t_ref[0, :, rows] + dtb_t_ref[...])
    la_t = -dt_t * jnp.exp(alog_t_ref[...])
    tri = jnp.where(causal, 1.0, 0.0).astype(BF16)
    tri_t = jnp.where(row_i <= col_j, 1.0, 0.0).astype(BF16)
    cum = _tri_dot(tri, la)
    cum_t = _dot_tri(la_t, tri_t)
    cum_last = cum[c - 1:c, :]
    q_scale = jnp.exp(cum)
    k_scale_ssd = dt * jnp.exp(cum_last - cum)
    s_scale = jnp.exp(cum_last)
    lane_half = lane // SSD_HEAD_DIM
    y_pairs = []
    for g in range(SSD_GROUPS):
        gm = lane_half == g
        cg = jnp.where(gm, cm, 0.0)
        bg = jnp.where(gm, bm, 0.0)
        cg_b = cg.astype(BF16)
        gmat = _dot_nt(cg_b, bg.astype(BF16))
        heads_per_group = SSD_HEADS // SSD_GROUPS
        for hh in range(heads_per_group):
            h = g * heads_per_group + hh
            pair, half = h // 2, h % 2
            x_pair = xs[:, 128 * pair:128 * pair + 128]
            xm = jnp.where(lane_half == half, x_pair, 0.0).astype(BF16)
            cc = cum[:, h:h + 1]
            cr = cum_t[h:h + 1, :]
            lmat = jnp.exp(jnp.where(causal, cc - cr, -jnp.inf)) * dt_t[h:h + 1, :]
            p = (gmat * lmat).astype(BF16)
            s_h = ssd_state[h]
            y = _dot(p, xm) + _dot((cg * q_scale[:, h:h + 1]).astype(BF16), s_h.astype(BF16))
            ssd_state[h] = s_h * s_scale[:, h:h + 1] + _dot_tn(bg * k_scale_ssd[:, h:h + 1], xm)
            if half == 0:
                y_pairs.append(y)
            else:
                y_pairs[pair] = y_pairs[pair] + y
    y = jnp.concatenate(y_pairs, axis=-1) + dskip_ref[...] * xs
    z = proj_ref[0, rows, 1536:2048].astype(F32)
    y = y * (z * _sigmoid(z))
    group = SSD_INNER // SSD_GROUPS
    for g in range(SSD_GROUPS):
        store_ssd(rows, slice(group * g, group * (g + 1)),
                  _rms(y[:, group * g:group * (g + 1)],
                       normw_ref[:, group * g:group * (g + 1)]).astype(BF16))


def _tile_maps(n_tiles, tiles_per_seq):
    def mixer3(i):
        tile = jnp.minimum(i, n_tiles - 1)
        return tile // tiles_per_seq, tile % tiles_per_seq, 0

    def mixer_t(i):
        tile = jnp.minimum(i, n_tiles - 1)
        return tile // tiles_per_seq, 0, tile % tiles_per_seq

    def mixer_pos(i):
        return jnp.minimum(i, n_tiles - 1) % tiles_per_seq, 0

    def ffn(i):
        return jnp.maximum(i - 1, 0), 0

    return mixer3, mixer_t, mixer_pos, ffn


def _ffn_weight_specs(half, j, layer):
    return [
        _layer_spec((half, D_MODEL), j, 0),
        _layer_spec((half, D_MODEL), j, 1),
        _const_spec((1, D_MODEL)),
        _layer_spec((D_MODEL, FFN_HIDDEN), layer),
        _layer_spec((D_MODEL, FFN_HIDDEN), layer),
        _layer_spec((FFN_HIDDEN, D_MODEL), layer),
    ]


def _even_ffn_kernel(proj_ref, gate_ref, gate_t_ref, cos_ref, sin_ref, gnw_ref,
                     convw_ref, convb_ref, dtb_ref, dtb_t_ref, alog_ref, alog_t_ref,
                     dskip_ref, normw_ref,
                     h_ref, woa_ref, wob_ref, nw_ref, wg_ref, wu_ref, wd_ref,
                     out_ref,
                     ret_state, ssd_state, conv_buf, xbc_buf, decay_tab, qdec_tab, kdec_tab,
                     alast_tab, yret_scr, yssd_scr, act_ref, *, n_tiles, tiles_per_seq):
    i = pl.program_id(0)
    tile = jnp.minimum(i, n_tiles - 1)
    slot = i % 2

    @pl.when(i == 0)
    def _():
        _even_tables(decay_tab, qdec_tab, kdec_tab, alast_tab)
        yret_scr[...] = jnp.zeros_like(yret_scr)
        yssd_scr[...] = jnp.zeros_like(yssd_scr)

    @pl.when(tile % tiles_per_seq == 0)
    def _():
        ret_state[...] = jnp.zeros_like(ret_state)
        ssd_state[...] = jnp.zeros_like(ssd_state)
        conv_buf[0:8, :] = jnp.zeros((8, SSD_CONV_DIM), F32)

    ya = yret_scr[1 - slot]
    yb = yssd_scr[1 - slot]

    def store_ret(rows, lanes, value):
        yret_scr[slot, rows, lanes] = value

    def store_ssd(rows, lanes, value):
        yssd_scr[slot, rows, lanes] = value

    chunks = [
        functools.partial(
            _even_chunk, ci * CHUNK, proj_ref, gate_ref, gate_t_ref, cos_ref, sin_ref, gnw_ref,
            convw_ref, convb_ref, dtb_ref, dtb_t_ref, alog_ref, alog_t_ref,
            dskip_ref, normw_ref, store_ret, store_ssd,
            ret_state, ssd_state, conv_buf, xbc_buf, decay_tab, qdec_tab, kdec_tab, alast_tab)
        for ci in range(ROW_TILE // CHUNK)]
    out_ref[...] = _ffn_tile(ya, yb, h_ref[...], woa_ref, wob_ref, nw_ref, wg_ref, wu_ref, wd_ref,
                             act_ref, chunks)


def _even_mixer_ffn(proj, gates, gates_t, cos, sin, gn_w, conv_w, conv_b, dt_b, dt_b_t,
                    a_log, a_log_t, d_skip, norm_w, h2d, w_out, j, ffn_norm_w, w_gate, w_up, w_down,
                    layer):
    b, t_len, _ = proj.shape
    c = CHUNK
    half = RET_HEADS * RET_DV
    tiles_per_seq = t_len // ROW_TILE
    n_tiles = b * tiles_per_seq
    mixer3, mixer_t, mixer_pos, ffn = _tile_maps(n_tiles, tiles_per_seq)
    kern = functools.partial(_even_ffn_kernel, n_tiles=n_tiles, tiles_per_seq=tiles_per_seq)
    return pl.pallas_call(
        kern,
        grid=(n_tiles + 1,),
        in_specs=[
            pl.BlockSpec((1, ROW_TILE, AB_MAIN), mixer3),
            pl.BlockSpec((1, ROW_TILE, GATE_PAD), mixer3),
            pl.BlockSpec((1, 8, ROW_TILE), mixer_t),
            pl.BlockSpec((ROW_TILE, V7X_LANES), mixer_pos),
            pl.BlockSpec((ROW_TILE, V7X_LANES), mixer_pos),
            _const_spec(gn_w.shape), _const_spec(conv_w.shape), _const_spec(conv_b.shape),
            _const_spec(dt_b.shape), _const_spec(dt_b_t.shape),
            _const_spec(a_log.shape), _const_spec(a_log_t.shape),
            _const_spec(d_skip.shape), _const_spec(norm_w.shape),
            pl.BlockSpec((ROW_TILE, D_MODEL), ffn),
        ] + _ffn_weight_specs(half, j, layer),
        out_specs=pl.BlockSpec((ROW_TILE, D_MODEL), ffn),
        out_shape=jax.ShapeDtypeStruct(h2d.shape, F32),
        scratch_shapes=[
            pltpu.VMEM((2 * V7X_LANES, RET_HEADS * RET_DV), F32),
            pltpu.VMEM((SSD_HEADS, V7X_LANES, V7X_LANES), F32),
            pltpu.VMEM((c + 8, SSD_CONV_DIM), F32),
            pltpu.VMEM((c, SSD_CONV_DIM), F32),
            pltpu.VMEM((RET_HEADS, c, c), F32),
            pltpu.VMEM((RET_HEADS, c, V7X_LANES), F32),
            pltpu.VMEM((c, V7X_LANES), F32),
            pltpu.VMEM((V7X_LANES, V7X_LANES), F32),
            pltpu.VMEM((2, ROW_TILE, half), BF16),
            pltpu.VMEM((2, ROW_TILE, SSD_INNER), BF16),
            pltpu.VMEM((ROW_TILE, FFN_HIDDEN), BF16),
        ],
        compiler_params=_params(("arbitrary",)),
        name="even_mixer_ffn",
    )(proj, gates, gates_t, cos, sin, gn_w, conv_w, conv_b, dt_b, dt_b_t,
      a_log, a_log_t, d_skip, norm_w, h2d, w_out, w_out, ffn_norm_w, w_gate, w_up, w_down)


def _hgrn2_block(proj_ref, lbl_ref, normw_ref, store_y, state_t, layer):
    c = CHUNK
    n_chunks = ROW_TILE // c

    logits = lbl_ref[...]
    e = jnp.exp(logits - jnp.max(logits, axis=0, keepdims=True))
    prob = e / jnp.sum(e, axis=0, keepdims=True)
    lb = jnp.zeros((1, HG_HEADS * HG_DIM), F32)
    for i in range(1, layer + 1):
        lb = lb + prob[i:i + 1, :]

    row_i = lax.broadcasted_iota(jnp.int32, (c, c), 0)
    col_j = lax.broadcasted_iota(jnp.int32, (c, c), 1)
    tri = jnp.where(row_i >= col_j, 1.0, 0.0).astype(BF16)

    width = HG_HEADS * HG_DIM

    carried = {"states": [None] * HG_HEADS}

    def head_step(ci, h):
        rows = slice(ci * c, (ci + 1) * c)
        if h == 0:
            zf = proj_ref[0, rows, width:2 * width].astype(F32)
            f = lb + (1.0 - lb) * _sigmoid(zf)
            carried["k_all"] = (1.0 - lb) * _sigmoid(-zf)
            carried["cum_all"] = _tri_dot(tri, jnp.log(f))
        k_all, cum_all = carried["k_all"], carried["cum_all"]
        sl = slice(HG_DIM * h, HG_DIM * (h + 1))
        q = proj_ref[0, rows, HG_DIM * h:HG_DIM * (h + 1)].astype(F32)
        k = k_all[:, sl]
        cum = cum_all[:, sl]
        v = proj_ref[0, rows, 2 * width + HG_DIM * h:2 * width + HG_DIM * (h + 1)]
        st = state_t[h] if ci == 0 else carried["states"][h]
        inter = _dot_nt((q * jnp.exp(cum)).astype(BF16), st.astype(BF16))
        n_blk = c // SUB
        ends = [cum[(j + 1) * SUB - 1:(j + 1) * SUB, :] for j in range(n_blk)]
        starts = [jnp.zeros((1, HG_DIM), F32)] + ends[:-1]
        start_rows = jnp.concatenate(
            [jnp.broadcast_to(s, (SUB, HG_DIM)) for s in starts], axis=0)
        end_rows = jnp.concatenate(
            [jnp.broadcast_to(e, (SUB, HG_DIM)) for e in ends], axis=0)
        q_blk = (q * jnp.exp(cum - start_rows)).astype(BF16)
        k_end = k * jnp.exp(end_rows - cum)
        k_end_b = k_end.astype(BF16)
        k_own = (k * jnp.exp(start_rows - cum)).astype(BF16)
        blocks = []
        for i in range(n_blk):
            pieces = []
            for j in range(i):
                if j == i - 1:
                    pieces.append(k_end_b[j * SUB:(j + 1) * SUB, :])
                else:
                    gap = jnp.exp(starts[i] - ends[j])
                    pieces.append((k_end[j * SUB:(j + 1) * SUB, :] * gap).astype(BF16))
            pieces.append(k_own[i * SUB:(i + 1) * SUB, :])
            if i + 1 < n_blk:
                pieces.append(jnp.zeros((c - (i + 1) * SUB, HG_DIM), BF16))
            kb = jnp.concatenate(pieces, axis=0)
            blocks.append(_dot_nt(q_blk[i * SUB:(i + 1) * SUB, :], kb))
        scores = jnp.where(row_i >= col_j, jnp.concatenate(blocks, axis=0), 0.0)
        o = inter + _dot(scores.astype(BF16), v)
        cl = cum[c - 1:c, :]
        k_out = (k * jnp.exp(cl - cum)).astype(BF16)
        new_state = st * jnp.exp(cl) + _dot_tn(v.astype(F32), k_out)
        if ci == n_chunks - 1:
            state_t[h] = new_state
        else:
            carried["states"][h] = new_state
        g = proj_ref[0, rows, 3 * width + HG_DIM * h:3 * width + HG_DIM * (h + 1)].astype(F32)
        store_y(rows, sl, (_rms(o, normw_ref[...]) * (g * _sigmoid(g))).astype(BF16))

    def chunk_step(ci):
        for h in range(HG_HEADS):
            head_step(ci, h)

    return [functools.partial(chunk_step, ci) for ci in range(n_chunks)]


def _hgrn2_ffn_kernel(proj_ref, lbl_ref, hnw_ref, yfox_ref, h_ref,
                      woa_ref, wob_ref, nw_ref, wg_ref, wu_ref, wd_ref,
                      out_ref, state_t, y_scr, act_ref, *, layer, n_tiles, tiles_per_seq):
    i = pl.program_id(0)
    tile = jnp.minimum(i, n_tiles - 1)
    slot = i % 2

    @pl.when(i == 0)
    def _():
        y_scr[...] = jnp.zeros_like(y_scr)

    @pl.when(tile % tiles_per_seq == 0)
    def _():
        state_t[...] = jnp.zeros_like(state_t)

    ya = y_scr[1 - slot]

    def store_y(rows, lanes, value):
        y_scr[slot, rows, lanes] = value

    chunks = _hgrn2_block(proj_ref, lbl_ref, hnw_ref, store_y, state_t, layer)
    out_ref[...] = _ffn_tile(ya, yfox_ref[...], h_ref[...], woa_ref, wob_ref, nw_ref,
                             wg_ref, wu_ref, wd_ref, act_ref, chunks)


def _hgrn2_ffn(proj, lb_logits, hg_norm_w, y_fox, h2d, w_out, j, ffn_norm_w, w_gate, w_up, w_down,
               layer):
    b, t_len, _ = proj.shape
    width = HG_HEADS * HG_DIM
    tiles_per_seq = t_len // ROW_TILE
    n_tiles = b * tiles_per_seq
    mixer3, _, _, ffn = _tile_maps(n_tiles, tiles_per_seq)
    kern = functools.partial(_hgrn2_ffn_kernel, layer=j, n_tiles=n_tiles,
                             tiles_per_seq=tiles_per_seq)
    return pl.pallas_call(
        kern,
        grid=(n_tiles + 1,),
        in_specs=[
            pl.BlockSpec((1, ROW_TILE, 4 * width), mixer3),
            _const_spec(lb_logits.shape),
            _const_spec(hg_norm_w.shape),
            pl.BlockSpec((ROW_TILE, width), ffn),
            pl.BlockSpec((ROW_TILE, D_MODEL), ffn),
        ] + _ffn_weight_specs(width, j, layer),
        out_specs=pl.BlockSpec((ROW_TILE, D_MODEL), ffn),
        out_shape=jax.ShapeDtypeStruct(h2d.shape, F32),
        scratch_shapes=[
            pltpu.VMEM((HG_HEADS, HG_DIM, HG_DIM), F32),
            pltpu.VMEM((2, ROW_TILE, width), BF16),
            pltpu.VMEM((ROW_TILE, FFN_HIDDEN), BF16),
        ],
        compiler_params=_params(("arbitrary",)),
        name="hgrn2_ffn",
    )(proj, lb_logits, hg_norm_w, y_fox, h2d, w_out, w_out, ffn_norm_w, w_gate, w_up, w_down)


def _forget_bias_kernel(gate_ref, bias_ref, out_ref):
    t_len = gate_ref.shape[1]
    blk = V7X_LANES
    row_i = lax.broadcasted_iota(jnp.int32, (blk, blk), 0)
    col_j = lax.broadcasted_iota(jnp.int32, (blk, blk), 1)
    tri = jnp.where(row_i >= col_j, 1.0, 0.0).astype(BF16)
    lane = lax.broadcasted_iota(jnp.int32, (1, V7X_LANES), 1)
    offset = jnp.zeros((1, V7X_LANES), F32)
    for i in range(t_len // blk):
        x = gate_ref[0, i * blk:(i + 1) * blk, :] + bias_ref[...]
        log_f = jnp.minimum(x, 0.0) - jnp.log1p(jnp.exp(-jnp.abs(x)))
        local = _tri_dot(tri, log_f)
        hi, mid, lo = _split3((local + offset) * -LOG2_E)
        offset = offset + local[blk - 1:blk, :]
        hi, mid, lo = hi.astype(F32), mid.astype(F32), lo.astype(F32)
        for h in range(FOX_HEADS):
            parts = jnp.where(lane == 0, hi[:, h:h + 1],
                              jnp.where(lane == 1, mid[:, h:h + 1],
                                        jnp.where(lane == 2, lo[:, h:h + 1], 0.0)))
            out_ref[0, i * blk:(i + 1) * blk, h * V7X_LANES:(h + 1) * V7X_LANES] = parts.astype(BF16)


def _forget_bias(gates, bias):
    b, t_len, _ = gates.shape
    return pl.pallas_call(
        _forget_bias_kernel,
        grid=(b,),
        in_specs=[pl.BlockSpec((1, t_len, GATE_PAD), lambda i: (i, 0, 0)), _const_spec(bias.shape)],
        out_specs=pl.BlockSpec((1, t_len, FOX_HEADS * V7X_LANES), lambda i: (i, 0, 0)),
        out_shape=jax.ShapeDtypeStruct((b, t_len, FOX_HEADS * V7X_LANES), BF16),
        compiler_params=_params(("parallel",)),
        name="forget_bias",
    )(gates, bias)


def _fox_kernel(q_ref, k_ref, kb_ref, v_ref, y_ref):
    sub, keys = ATT_SUB, ATT_KEYS
    n_sub = ATT_ROWS // sub
    qi = pl.program_id(2)
    q0 = pl.multiple_of(qi * ATT_ROWS, ATT_ROWS)
    lane = lax.broadcasted_iota(jnp.int32, (sub, V7X_LANES), 1)
    q_ones = jnp.where(lane < 3, 1.0, 0.0).astype(BF16)

    def tile(state, r, k, v, mask):
        m, acc = state
        q = jnp.concatenate([q_ref[0, r * sub:(r + 1) * sub, :], q_ones], axis=-1)
        s = _dot_nt(q, k)
        if mask is not None:
            s = jnp.where(mask, s, -jnp.inf)
        m_new = jnp.maximum(m, jnp.max(s, axis=-1, keepdims=True))
        alpha = jnp.exp2(m - m_new)
        p = jnp.exp2(s - m_new)
        acc = alpha * acc + _dot(p.astype(BF16), v)
        return m_new, acc

    def load(k0, width):
        k = jnp.concatenate([k_ref[0, pl.ds(k0, width), :], kb_ref[0, pl.ds(k0, width), :]], axis=-1)
        v = jnp.concatenate([v_ref[0, pl.ds(k0, width), :], jnp.ones((width, V7X_LANES), BF16)],
                            axis=-1)
        return k, v

    def before(kj, states):
        k, v = load(pl.multiple_of(kj * keys, keys), keys)
        return tuple(tile(states[r], r, k, v, None) for r in range(n_sub))

    init = tuple((jnp.full((sub, 1), -jnp.inf, F32), jnp.zeros((sub, 2 * FOX_DIM), F32))
                 for _ in range(n_sub))
    states = list(lax.fori_loop(0, qi * (ATT_ROWS // keys), before, init))

    row_i = lax.broadcasted_iota(jnp.int32, (sub, sub), 0)
    col_j = lax.broadcasted_iota(jnp.int32, (sub, sub), 1)
    causal = row_i >= col_j
    for c in range(n_sub):
        k, v = load(pl.multiple_of(q0 + c * sub, sub), sub)
        for r in range(c, n_sub):
            states[r] = tile(states[r], r, k, v, causal if r == c else None)
    for r in range(n_sub):
        _, acc = states[r]
        y_ref[0, r * sub:(r + 1) * sub, :] = (acc[:, 0:FOX_DIM] / acc[:, FOX_DIM:]).astype(y_ref.dtype)


def _fox(proj, key_bias):
    b, t_len, _ = proj.shape
    blk = ATT_ROWS
    qcol, kcol, vcol = (2048 // FOX_DIM, 2560 // FOX_DIM, 3072 // FOX_DIM)
    return pl.pallas_call(
        _fox_kernel,
        grid=(b, FOX_HEADS, t_len // blk),
        in_specs=[
            pl.BlockSpec((1, blk, FOX_DIM), lambda i, h, j: (i, j, qcol + h)),
            pl.BlockSpec((1, t_len, FOX_DIM), lambda i, h, j: (i, 0, kcol + h)),
            pl.BlockSpec((1, t_len, V7X_LANES), lambda i, h, j: (i, 0, h)),
            pl.BlockSpec((1, t_len, FOX_DIM), lambda i, h, j: (i, 0, vcol + h)),
        ],
        out_specs=pl.BlockSpec((1, blk, FOX_DIM), lambda i, h, j: (i, j, h)),
        out_shape=jax.ShapeDtypeStruct((b, t_len, FOX_HEADS * FOX_DIM), BF16),
        compiler_params=_params(("parallel", "parallel", "arbitrary")),
        name="fox_attention",
    )(proj, proj, key_bias, proj)


def _rotary_perm():
    width = RET_HEADS * RET_DK
    perm = np.zeros((2 * width,), np.int32)
    for base in (0, width):
        for half in range(2):
            for h in range(RET_HEADS):
                for i in range(RET_DK // 2):
                    perm[base + half * 128 + h * 32 + i] = base + h * RET_DK + 2 * i + half
    return perm


def _pad_lanes(a, width=GATE_PAD):
    return jnp.pad(a, [(0, 0)] * (a.ndim - 1) + [(0, width - a.shape[-1])])


def kernel(x, norm_mix, norm_ffn, ffn_w_gate, ffn_w_up, ffn_w_down, ab_w_in, ab_w_out, ret_gn_w,
           ssd_conv_w, ssd_conv_b, ssd_dt_bias, ssd_a_log, ssd_d, ssd_norm_w, cd_w_in, cd_w_out,
           hg_lb_logits, hg_norm_w, fox_f_bias, fox_q_norm_w, fox_k_norm_w):
    b, t_len, d = x.shape
    depth = norm_mix.shape[0]
    n_rows = b * t_len
    perm = _rotary_perm()

    half = RET_DK // 2
    freqs = ROPE_BASE ** (-jnp.linspace(0.0, 1.0, half, dtype=F32))
    ang = jnp.arange(t_len, dtype=F32)[:, None] * freqs[None, :]
    cos = jnp.tile(jnp.cos(ang), (1, RET_HEADS))
    sin = jnp.tile(jnp.sin(ang), (1, RET_HEADS))

    ab_w_out_b, cd_w_out_b = ab_w_out.astype(BF16), cd_w_out.astype(BF16)
    w_gate_b, w_up_b, w_down_b = (ffn_w_gate.astype(BF16), ffn_w_up.astype(BF16),
                                  ffn_w_down.astype(BF16))

    h2d = x.reshape(n_rows, d)
    for layer in range(depth):
        j = layer // 2
        if layer % 2 == 0:
            w_qk = ab_w_in[j, :, 0:2 * RET_HEADS * RET_DK][:, perm]
            proj, gates = _inproj(h2d, norm_mix[layer][None, :], ab_w_in, j, AB_MAIN, w_qk=w_qk)
            gates3 = gates.reshape(b, t_len, GATE_PAD)
            gates_t = jnp.transpose(gates3[:, :, 0:8], (0, 2, 1))
            h2d = _even_mixer_ffn(
                proj.reshape(b, t_len, AB_MAIN), gates3, gates_t, cos, sin,
                ret_gn_w[j].reshape(1, RET_HEADS * RET_DV),
                ssd_conv_w[j], ssd_conv_b[j][None, :],
                _pad_lanes(ssd_dt_bias[j][None, :]), ssd_dt_bias[j][:, None],
                _pad_lanes(ssd_a_log[j][None, :]), ssd_a_log[j][:, None],
                jnp.repeat(ssd_d[j], SSD_HEAD_DIM)[None, :], ssd_norm_w[j][None, :],
                h2d, ab_w_out_b, j, norm_ffn[layer][None, :], w_gate_b, w_up_b, w_down_b, layer)
        else:
            qk_w = jnp.concatenate([jnp.tile(fox_q_norm_w[j] * (FOX_DIM ** -0.5 * LOG2_E), FOX_HEADS),
                                    jnp.tile(fox_k_norm_w[j], FOX_HEADS)])[None, :]
            proj, gates = _inproj(h2d, norm_mix[layer][None, :], cd_w_in, j, CD_MAIN,
                                  qk_w=qk_w, norm_cols=(2048, 3072))
            proj3 = proj.reshape(b, t_len, CD_MAIN)
            key_bias = _forget_bias(gates.reshape(b, t_len, GATE_PAD),
                                    _pad_lanes(fox_f_bias[j][None, :]))
            y_fox = _fox(proj3, key_bias).reshape(n_rows, FOX_HEADS * FOX_DIM)
            h2d = _hgrn2_ffn(proj3, hg_lb_logits, hg_norm_w[j][None, :], y_fox, h2d, cd_w_out_b, j,
                             norm_ffn[layer][None, :], w_gate_b, w_up_b, w_down_b, layer)
    return h2d.reshape(b, t_len, d)
```

```python
import functools
import math

import numpy as np
import jax
import jax.numpy as jnp
from jax import lax
from jax.experimental import pallas as pl
from jax.experimental.pallas import tpu as pltpu

F32 = jnp.float32
BF16 = jnp.bfloat16

D_MODEL = 1024
EPS = 1e-6
ROPE_BASE = 10000.0
LOG2_E = 1.4426950408889634

RET_HEADS, RET_DK, RET_DV = 4, 64, 128
SSD_HEADS, SSD_HEAD_DIM, SSD_GROUPS, SSD_STATE, SSD_CONV = 8, 64, 2, 64, 4
SSD_INNER = SSD_HEADS * SSD_HEAD_DIM
SSD_CONV_DIM = SSD_INNER + 2 * SSD_GROUPS * SSD_STATE
HG_HEADS, HG_DIM = 4, 128
FOX_HEADS, FOX_DIM = 4, 128
FFN_HIDDEN = 2816

AB_MAIN = 2816
AB_GATES = SSD_HEADS
CD_MAIN = 3584
CD_GATES = FOX_HEADS
GATE_PAD = 128

V7X_LANES = 128
V7X_VMEM_LIMIT = 56 * 1024 * 1024

ROW_TILE = 512
COL_TILE = 512
FFN_TILE = 256
CHUNK = 128
SUB = 16
ATT_ROWS = 1024
ATT_SUB = 256
ATT_KEYS = 1024


def _dot(a, b):
    return jnp.dot(a, b, preferred_element_type=F32)


def _dot_nt(a, b):
    return lax.dot_general(a, b, (((1,), (1,)), ((), ())), preferred_element_type=F32)


def _dot_tn(a, b):
    return _dot(a.T.astype(BF16), b)


def _split3(x):
    hi = x.astype(BF16)
    r = x - hi.astype(F32)
    mid = r.astype(BF16)
    lo = (r - mid.astype(F32)).astype(BF16)
    return hi, mid, lo


def _tri_dot(tri, x):
    hi, mid, lo = _split3(x)
    return _dot(tri, hi) + _dot(tri, mid) + _dot(tri, lo)


def _dot_tri(x, tri):
    hi, mid, lo = _split3(x)
    return _dot(hi, tri) + _dot(mid, tri) + _dot(lo, tri)


def _sigmoid(x):
    return 1.0 / (1.0 + jnp.exp(-x))


def _softplus(x):
    return jnp.maximum(x, 0.0) + jnp.log1p(jnp.exp(-jnp.abs(x)))


def _rms(x, w):
    return x * lax.rsqrt(jnp.mean(x * x, axis=-1, keepdims=True) + EPS) * w


def _const_spec(shape):
    zeros = (0,) * len(shape)
    return pl.BlockSpec(shape, lambda *_: zeros, pipeline_mode=pl.Buffered(1))


def _params(semantics):
    return pltpu.CompilerParams(dimension_semantics=semantics,
                                vmem_limit_bytes=V7X_VMEM_LIMIT)


def _inproj_kernel(h_ref, nw_ref, w_ref, wqk_ref, qkw_ref, out_ref, gate_ref, wb_ref, *,
                   n_main, norm_cols):
    n_all = w_ref.shape[2]

    @pl.when(pl.program_id(0) == 0)
    def _():
        for c0 in range(0, n_main, COL_TILE):
            width = min(COL_TILE, n_main - c0)
            wb_ref[:, c0:c0 + width] = w_ref[0, :, c0:c0 + width].astype(BF16)
        wb_ref[:, n_main:n_main + GATE_PAD] = jnp.zeros((D_MODEL, GATE_PAD), BF16)
        wb_ref[:, n_main:n_all] = w_ref[0, :, n_main:n_all].astype(BF16)
        if wqk_ref is not None:
            wb_ref[:, 0:wqk_ref.shape[1]] = wqk_ref[...].astype(BF16)

    x = h_ref[...]
    ub = _rms(x, nw_ref[...]).astype(BF16)
    for c0 in range(0, n_main, COL_TILE):
        width = min(COL_TILE, n_main - c0)
        y = _dot(ub, wb_ref[:, c0:c0 + width])
        for l0 in range(0, width, V7X_LANES):
            col = c0 + l0
            piece = y[:, l0:l0 + V7X_LANES]
            if norm_cols[0] <= col < norm_cols[1]:
                piece = _rms(piece, qkw_ref[:, col - norm_cols[0]:col - norm_cols[0] + V7X_LANES])
            out_ref[:, col:col + V7X_LANES] = piece.astype(out_ref.dtype)
    gate_ref[...] = _dot(ub, wb_ref[:, n_main:n_main + GATE_PAD])


def _inproj(h2d, norm_w, w_all, layer, n_main, w_qk=None, qk_w=None, norm_cols=(0, 0)):
    n_rows = h2d.shape[0]
    n_all = w_all.shape[2]

    def body(h_ref, nw_ref, w_ref, *rest):
        rest = list(rest)
        wqk_ref = rest.pop(0) if w_qk is not None else None
        qkw_ref = rest.pop(0) if qk_w is not None else None
        _inproj_kernel(h_ref, nw_ref, w_ref, wqk_ref, qkw_ref, *rest,
                       n_main=n_main, norm_cols=norm_cols)

    operands = [h2d, norm_w, w_all]
    in_specs = [
        pl.BlockSpec((ROW_TILE, D_MODEL), lambda i: (i, 0)),
        _const_spec((1, D_MODEL)),
        pl.BlockSpec((1, D_MODEL, n_all), lambda i: (layer, 0, 0), pipeline_mode=pl.Buffered(1)),
    ]
    for extra in (w_qk, qk_w):
        if extra is not None:
            operands.append(extra)
            in_specs.append(_const_spec(extra.shape))
    return pl.pallas_call(
        body,
        grid=(n_rows // ROW_TILE,),
        in_specs=in_specs,
        out_specs=[
            pl.BlockSpec((ROW_TILE, n_main), lambda i: (i, 0)),
            pl.BlockSpec((ROW_TILE, GATE_PAD), lambda i: (i, 0)),
        ],
        out_shape=[
            jax.ShapeDtypeStruct((n_rows, n_main), BF16),
            jax.ShapeDtypeStruct((n_rows, GATE_PAD), F32),
        ],
        scratch_shapes=[pltpu.VMEM((D_MODEL, n_main + GATE_PAD), BF16)],
        compiler_params=_params(("arbitrary",)),
        name="inproj",
    )(*operands)


def _ffn_tile(ya, yb, h, woa_ref, wob_ref, nw_ref, wg_ref, wu_ref, wd_ref, act_ref, out_ref,
              between):
    between = iter(between)
    next(between, None)
    h1 = h + _dot(ya, woa_ref[0]) + _dot(yb, wob_ref[0])
    ub = _rms(h1, nw_ref[...]).astype(BF16)
    for c0 in range(0, FFN_HIDDEN, FFN_TILE):
        next(between, None)
        g = _dot(ub, wg_ref[0, :, c0:c0 + FFN_TILE])
        up = _dot(ub, wu_ref[0, :, c0:c0 + FFN_TILE])
        act_ref[:, c0:c0 + FFN_TILE] = (g * _sigmoid(g) * up).astype(BF16)
    for n0 in range(0, D_MODEL, FFN_TILE):
        next(between, None)
        out_ref[:, n0:n0 + FFN_TILE] = (
            h1[:, n0:n0 + FFN_TILE] + _dot(act_ref[...], wd_ref[0, :, n0:n0 + FFN_TILE]))
    for _ in between:
        pass


def _layer_spec(shape, layer, row_block=0):
    return pl.BlockSpec((1,) + shape, lambda i: (layer, row_block, 0), pipeline_mode=pl.Buffered(1))


def _even_tables(decay_tab, qdec_tab, kdec_tab, alast_tab):
    c = CHUNK
    row_i = lax.broadcasted_iota(jnp.int32, (c, c), 0)
    col_j = lax.broadcasted_iota(jnp.int32, (c, c), 1)
    lane = lax.broadcasted_iota(jnp.int32, (1, V7X_LANES), 1)
    log_gammas = [math.log1p(-(2.0 ** (-5 - h))) for h in range(RET_HEADS)]
    dij = (row_i - col_j).astype(F32)
    pos = lax.broadcasted_iota(jnp.int32, (c, V7X_LANES), 0).astype(F32)
    for h, log_g in enumerate(log_gammas):
        decay_tab[h] = jnp.exp(jnp.where(row_i >= col_j, dij * log_g, -jnp.inf))
        qdec_tab[h] = jnp.exp((pos + 1.0) * log_g)
    lane_log_g = jnp.zeros((1, V7X_LANES), F32)
    row_head = lax.broadcasted_iota(jnp.int32, (V7X_LANES, V7X_LANES), 0) // (RET_DK // 2)
    row_log_g = jnp.zeros((V7X_LANES, V7X_LANES), F32)
    for h, log_g in enumerate(log_gammas):
        lane_log_g = jnp.where(lane // (RET_DK // 2) == h, log_g, lane_log_g)
        row_log_g = jnp.where(row_head == h, log_g, row_log_g)
    kdec_tab[...] = jnp.exp((float(c - 1) - pos) * lane_log_g) * (RET_DK ** -0.5)
    alast_tab[...] = jnp.exp(float(c) * row_log_g)


def _zip_stages(*generators):
    live = [iter(g) for g in generators]
    while live:
        still = []
        for g in live:
            try:
                next(g)
                still.append(g)
            except StopIteration:
                pass
        live = still
        if live:
            yield


def _retention_stages(r0, proj_ref, cos_ref, sin_ref, gnw_ref, store_ret,
                      ret_state, decay_tab, qdec_tab, kdec_tab, alast_tab):
    c = CHUNK
    rows = slice(r0, r0 + c)
    lane = lax.broadcasted_iota(jnp.int32, (1, V7X_LANES), 1)

    cos = cos_ref[rows, :]
    sin = sin_ref[rows, :]
    q1 = proj_ref[0, rows, 0:128].astype(F32)
    q2 = proj_ref[0, rows, 128:256].astype(F32)
    k1 = proj_ref[0, rows, 256:384].astype(F32)
    k2 = proj_ref[0, rows, 384:512].astype(F32)
    rq1 = q1 * cos - q2 * sin
    rq2 = q1 * sin + q2 * cos
    k_scale = RET_DK ** -0.5
    rk1 = k1 * cos - k2 * sin
    rk2 = k1 * sin + k2 * cos
    k_t = (jnp.concatenate([rk1.T, rk2.T], axis=0) * k_scale).astype(BF16)
    lane_head = lane // (RET_DK // 2)
    kdec = kdec_tab[...]
    k_out1 = (rk1 * kdec).T.astype(BF16)
    k_out2 = (rk2 * kdec).T.astype(BF16)
    qms = []
    for h in range(RET_HEADS):
        hm = lane_head == h
        qms.append(jnp.concatenate([jnp.where(hm, rq1, 0.0), jnp.where(hm, rq2, 0.0)],
                                   axis=-1).astype(BF16))
    yield
    scores = [_dot(qms[h], k_t) for h in range(RET_HEADS)]
    inters = [_dot(qms[h], ret_state[:, 128 * h:128 * h + 128].astype(BF16))
              for h in range(RET_HEADS)]
    v_all = proj_ref[0, rows, 512:1024]
    upd = jnp.concatenate([_dot(k_out1, v_all), _dot(k_out2, v_all)], axis=0)
    yield
    alast = alast_tab[...]
    for s0 in range(0, 2 * V7X_LANES, V7X_LANES):
        for c0 in range(0, RET_HEADS * RET_DV, V7X_LANES):
            ret_state[s0:s0 + V7X_LANES, c0:c0 + V7X_LANES] = (
                ret_state[s0:s0 + V7X_LANES, c0:c0 + V7X_LANES] * alast
                + upd[s0:s0 + V7X_LANES, c0:c0 + V7X_LANES])
    intras = [_dot((scores[h] * decay_tab[h]).astype(BF16),
                   proj_ref[0, rows, 512 + 128 * h:640 + 128 * h]) for h in range(RET_HEADS)]
    yield
    for h in range(RET_HEADS):
        o = intras[h] + inters[h] * qdec_tab[h]
        mu = jnp.mean(o, axis=-1, keepdims=True)
        d = o - mu
        var = jnp.mean(d * d, axis=-1, keepdims=True)
        on = d * lax.rsqrt(var + EPS) * gnw_ref[:, 128 * h:128 * h + 128]
        g = proj_ref[0, rows, 1024 + 128 * h:1152 + 128 * h].astype(F32)
        store_ret(rows, slice(128 * h, 128 * h + 128), (on * (g * _sigmoid(g))).astype(BF16))


def _ssd_stages(r0, proj_ref, gate_ref, gate_t_ref, convw_ref, convb_ref, dtb_ref, dtb_t_ref,
                alog_ref, alog_t_ref, dskip_ref, normw_ref, store_ssd,
                ssd_state, conv_buf, xbc_buf):
    c = CHUNK
    rows = slice(r0, r0 + c)
    row_i = lax.broadcasted_iota(jnp.int32, (c, c), 0)
    col_j = lax.broadcasted_iota(jnp.int32, (c, c), 1)
    causal = row_i >= col_j
    lane = lax.broadcasted_iota(jnp.int32, (1, V7X_LANES), 1)

    conv_buf[8:8 + c, :] = proj_ref[0, rows, 2048:2816].astype(F32)
    acc = jnp.zeros((c, SSD_CONV_DIM), F32) + convb_ref[...]
    for k in range(SSD_CONV):
        off = 8 - (SSD_CONV - 1) + k
        acc = acc + convw_ref[k:k + 1, :] * conv_buf[off:off + c, :]
    conv_buf[0:8, :] = conv_buf[c:c + 8, :]
    xbc_buf[...] = acc * _sigmoid(acc)
    xs = xbc_buf[:, 0:SSD_INNER]
    bm = xbc_buf[:, 512:640]
    cm = xbc_buf[:, 640:768]

    dt = _softplus(gate_ref[0, rows, :] + dtb_ref[...])
    la = -dt * jnp.exp(alog_ref[...])
    dt_t = _softplus(gate_t_ref[0, :, rows] + dtb_t_ref[...])
    la_t = -dt_t * jnp.exp(alog_t_ref[...])
    tri = jnp.where(causal, 1.0, 0.0).astype(BF16)
    tri_t = jnp.where(row_i <= col_j, 1.0, 0.0).astype(BF16)
    cum = _tri_dot(tri, la)
    cum_t = _dot_tri(la_t, tri_t)
    lane_half = lane // SSD_HEAD_DIM
    heads_per_group = SSD_HEADS // SSD_GROUPS
    cgs, bgs, gmats, xms = [], [], [], []
    for g in range(SSD_GROUPS):
        gm = lane_half == g
        cgs.append(jnp.where(gm, cm, 0.0))
        bgs.append(jnp.where(gm, bm, 0.0))
        gmats.append(_dot_nt(cgs[g].astype(BF16), bgs[g].astype(BF16)))
    for h in range(SSD_HEADS):
        pair, half = h // 2, h % 2
        xms.append(jnp.where(lane_half == half, xs[:, 128 * pair:128 * pair + 128], 0.0).astype(BF16))
    yield
    cum_last = cum[c - 1:c, :]
    q_scale = jnp.exp(cum)
    k_scale_ssd = dt * jnp.exp(cum_last - cum)
    s_scale = jnp.exp(cum_last)
    inters, updates, probs = [], [], []
    for h in range(SSD_HEADS):
        g = h // heads_per_group
        s_h = ssd_state[h]
        inters.append(_dot((cgs[g] * q_scale[:, h:h + 1]).astype(BF16), s_h.astype(BF16)))
        updates.append(s_h * s_scale[:, h:h + 1] + _dot_tn(bgs[g] * k_scale_ssd[:, h:h + 1], xms[h]))
        cc = cum[:, h:h + 1]
        cr = cum_t[h:h + 1, :]
        lmat = jnp.exp(jnp.where(causal, cc - cr, -jnp.inf)) * dt_t[h:h + 1, :]
        probs.append((gmats[g] * lmat).astype(BF16))
    yield
    intras = [_dot(probs[h], xms[h]) for h in range(SSD_HEADS)]
    for h in range(SSD_HEADS):
        ssd_state[h] = updates[h]
    yield
    y_pairs = [intras[2 * pr] + inters[2 * pr] + intras[2 * pr + 1] + inters[2 * pr + 1]
               for pr in range(SSD_HEADS // 2)]
    y = jnp.concatenate(y_pairs, axis=-1) + dskip_ref[...] * xbc_buf[:, 0:SSD_INNER]
    z = proj_ref[0, rows, 1536:2048].astype(F32)
    y = y * (z * _sigmoid(z))
    group = SSD_INNER // SSD_GROUPS
    for g in range(SSD_GROUPS):
        store_ssd(rows, slice(group * g, group * (g + 1)),
                  _rms(y[:, group * g:group * (g + 1)],
                       normw_ref[:, group * g:group * (g + 1)]).astype(BF16))


def _tile_maps(n_tiles, tiles_per_seq):
    def mixer3(i):
        tile = jnp.minimum(i, n_tiles - 1)
        return tile // tiles_per_seq, tile % tiles_per_seq, 0

    def mixer_t(i):
        tile = jnp.minimum(i, n_tiles - 1)
        return tile // tiles_per_seq, 0, tile % tiles_per_seq

    def mixer_pos(i):
        return jnp.minimum(i, n_tiles - 1) % tiles_per_seq, 0

    def ffn(i):
        return jnp.maximum(i - 1, 0), 0

    return mixer3, mixer_t, mixer_pos, ffn


def _ffn_weight_specs(half, j, layer):
    return [
        _layer_spec((half, D_MODEL), j, 0),
        _layer_spec((half, D_MODEL), j, 1),
        _const_spec((1, D_MODEL)),
        _layer_spec((D_MODEL, FFN_HIDDEN), layer),
        _layer_spec((D_MODEL, FFN_HIDDEN), layer),
        _layer_spec((FFN_HIDDEN, D_MODEL), layer),
    ]


def _even_ffn_kernel(proj_ref, gate_ref, gate_t_ref, cos_ref, sin_ref, gnw_ref,
                     convw_ref, convb_ref, dtb_ref, dtb_t_ref, alog_ref, alog_t_ref,
                     dskip_ref, normw_ref,
                     h_ref, woa_ref, wob_ref, nw_ref, wg_ref, wu_ref, wd_ref,
                     out_ref,
                     ret_state, ssd_state, conv_buf, xbc_buf, decay_tab, qdec_tab, kdec_tab,
                     alast_tab, yret_scr, yssd_scr, act_ref, *, n_tiles, tiles_per_seq):
    i = pl.program_id(0)
    tile = jnp.minimum(i, n_tiles - 1)
    slot = i % 2

    @pl.when(i == 0)
    def _():
        _even_tables(decay_tab, qdec_tab, kdec_tab, alast_tab)
        yret_scr[...] = jnp.zeros_like(yret_scr)
        yssd_scr[...] = jnp.zeros_like(yssd_scr)

    @pl.when(tile % tiles_per_seq == 0)
    def _():
        ret_state[...] = jnp.zeros_like(ret_state)
        ssd_state[...] = jnp.zeros_like(ssd_state)
        conv_buf[0:8, :] = jnp.zeros((8, SSD_CONV_DIM), F32)

    ya = yret_scr[1 - slot]
    yb = yssd_scr[1 - slot]

    def store_ret(rows, lanes, value):
        yret_scr[slot, rows, lanes] = value

    def store_ssd(rows, lanes, value):
        yssd_scr[slot, rows, lanes] = value

    def stages():
        for ci in range(ROW_TILE // CHUNK):
            yield from _zip_stages(
                _retention_stages(ci * CHUNK, proj_ref, cos_ref, sin_ref, gnw_ref, store_ret,
                                  ret_state, decay_tab, qdec_tab, kdec_tab, alast_tab),
                _ssd_stages(ci * CHUNK, proj_ref, gate_ref, gate_t_ref, convw_ref, convb_ref,
                            dtb_ref, dtb_t_ref, alog_ref, alog_t_ref, dskip_ref, normw_ref,
                            store_ssd, ssd_state, conv_buf, xbc_buf))
            yield

    _ffn_tile(ya, yb, h_ref[...], woa_ref, wob_ref, nw_ref, wg_ref, wu_ref, wd_ref,
              act_ref, out_ref, stages())


def _even_mixer_ffn(proj, gates, gates_t, cos, sin, gn_w, conv_w, conv_b, dt_b, dt_b_t,
                    a_log, a_log_t, d_skip, norm_w, h2d, w_out, j, ffn_norm_w, w_gate, w_up, w_down,
                    layer):
    b, t_len, _ = proj.shape
    c = CHUNK
    half = RET_HEADS * RET_DV
    tiles_per_seq = t_len // ROW_TILE
    n_tiles = b * tiles_per_seq
    mixer3, mixer_t, mixer_pos, ffn = _tile_maps(n_tiles, tiles_per_seq)
    kern = functools.partial(_even_ffn_kernel, n_tiles=n_tiles, tiles_per_seq=tiles_per_seq)
    return pl.pallas_call(
        kern,
        grid=(n_tiles + 1,),
        in_specs=[
            pl.BlockSpec((1, ROW_TILE, AB_MAIN), mixer3),
            pl.BlockSpec((1, ROW_TILE, GATE_PAD), mixer3),
            pl.BlockSpec((1, 8, ROW_TILE), mixer_t),
            pl.BlockSpec((ROW_TILE, V7X_LANES), mixer_pos),
            pl.BlockSpec((ROW_TILE, V7X_LANES), mixer_pos),
            _const_spec(gn_w.shape), _const_spec(conv_w.shape), _const_spec(conv_b.shape),
            _const_spec(dt_b.shape), _const_spec(dt_b_t.shape),
            _const_spec(a_log.shape), _const_spec(a_log_t.shape),
            _const_spec(d_skip.shape), _const_spec(norm_w.shape),
            pl.BlockSpec((ROW_TILE, D_MODEL), ffn),
        ] + _ffn_weight_specs(half, j, layer),
        out_specs=pl.BlockSpec((ROW_TILE, D_MODEL), ffn),
        out_shape=jax.ShapeDtypeStruct(h2d.shape, F32),
        scratch_shapes=[
            pltpu.VMEM((2 * V7X_LANES, RET_HEADS * RET_DV), F32),
            pltpu.VMEM((SSD_HEADS, V7X_LANES, V7X_LANES), F32),
            pltpu.VMEM((c + 8, SSD_CONV_DIM), F32),
            pltpu.VMEM((c, SSD_CONV_DIM), F32),
            pltpu.VMEM((RET_HEADS, c, c), F32),
            pltpu.VMEM((RET_HEADS, c, V7X_LANES), F32),
            pltpu.VMEM((c, V7X_LANES), F32),
            pltpu.VMEM((V7X_LANES, V7X_LANES), F32),
            pltpu.VMEM((2, ROW_TILE, half), BF16),
            pltpu.VMEM((2, ROW_TILE, SSD_INNER), BF16),
            pltpu.VMEM((ROW_TILE, FFN_HIDDEN), BF16),
        ],
        compiler_params=_params(("arbitrary",)),
        name="even_mixer_ffn",
    )(proj, gates, gates_t, cos, sin, gn_w, conv_w, conv_b, dt_b, dt_b_t,
      a_log, a_log_t, d_skip, norm_w, h2d, w_out, w_out, ffn_norm_w, w_gate, w_up, w_down)


def _hgrn2_block(proj_ref, lbl_ref, normw_ref, store_y, state_t, layer):
    c = CHUNK
    n_chunks = ROW_TILE // c

    logits = lbl_ref[...]
    e = jnp.exp(logits - jnp.max(logits, axis=0, keepdims=True))
    prob = e / jnp.sum(e, axis=0, keepdims=True)
    lb = jnp.zeros((1, HG_HEADS * HG_DIM), F32)
    for i in range(1, layer + 1):
        lb = lb + prob[i:i + 1, :]

    row_i = lax.broadcasted_iota(jnp.int32, (c, c), 0)
    col_j = lax.broadcasted_iota(jnp.int32, (c, c), 1)
    tri = jnp.where(row_i >= col_j, 1.0, 0.0).astype(BF16)

    width = HG_HEADS * HG_DIM

    carried = {"states": [None] * HG_HEADS}

    def gates(ci):
        rows = slice(ci * c, (ci + 1) * c)
        zf = proj_ref[0, rows, width:2 * width].astype(F32)
        f = lb + (1.0 - lb) * _sigmoid(zf)
        carried["k_all"] = (1.0 - lb) * _sigmoid(-zf)
        carried["cum_all"] = _tri_dot(tri, jnp.log(f))

    def head_scores(ci, h):
        rows = slice(ci * c, (ci + 1) * c)
        k_all, cum_all = carried["k_all"], carried["cum_all"]
        sl = slice(HG_DIM * h, HG_DIM * (h + 1))
        q = proj_ref[0, rows, HG_DIM * h:HG_DIM * (h + 1)].astype(F32)
        k = k_all[:, sl]
        cum = cum_all[:, sl]
        v = proj_ref[0, rows, 2 * width + HG_DIM * h:2 * width + HG_DIM * (h + 1)]
        st = state_t[h] if ci == 0 else carried["states"][h]
        inter = _dot_nt((q * jnp.exp(cum)).astype(BF16), st.astype(BF16))
        n_blk = c // SUB
        ends = [cum[(j + 1) * SUB - 1:(j + 1) * SUB, :] for j in range(n_blk)]
        starts = [jnp.zeros((1, HG_DIM), F32)] + ends[:-1]
        start_rows = jnp.concatenate(
            [jnp.broadcast_to(s, (SUB, HG_DIM)) for s in starts], axis=0)
        end_rows = jnp.concatenate(
            [jnp.broadcast_to(e, (SUB, HG_DIM)) for e in ends], axis=0)
        q_blk = (q * jnp.exp(cum - start_rows)).astype(BF16)
        k_end = k * jnp.exp(end_rows - cum)
        k_end_b = k_end.astype(BF16)
        k_own = (k * jnp.exp(start_rows - cum)).astype(BF16)
        blocks = []
        for i in range(n_blk):
            pieces = []
            for j in range(i):
                if j == i - 1:
                    pieces.append(k_end_b[j * SUB:(j + 1) * SUB, :])
                else:
                    gap = jnp.exp(starts[i] - ends[j])
                    pieces.append((k_end[j * SUB:(j + 1) * SUB, :] * gap).astype(BF16))
            pieces.append(k_own[i * SUB:(i + 1) * SUB, :])
            if i + 1 < n_blk:
                pieces.append(jnp.zeros((c - (i + 1) * SUB, HG_DIM), BF16))
            kb = jnp.concatenate(pieces, axis=0)
            blocks.append(_dot_nt(q_blk[i * SUB:(i + 1) * SUB, :], kb))
        scores = jnp.where(row_i >= col_j, jnp.concatenate(blocks, axis=0), 0.0).astype(BF16)
        cl = cum[c - 1:c, :]
        k_out = (k * jnp.exp(cl - cum)).astype(BF16)
        return rows, sl, h, inter, scores, v, st * jnp.exp(cl), k_out

    def head_output(ci, rows, sl, h, inter, scores, v, st_decayed, k_out):
        o = inter + _dot(scores, v)
        new_state = st_decayed + _dot_tn(v.astype(F32), k_out)
        if ci == n_chunks - 1:
            state_t[h] = new_state
        else:
            carried["states"][h] = new_state
        g = proj_ref[0, rows, 3 * width + HG_DIM * h:3 * width + HG_DIM * (h + 1)].astype(F32)
        store_y(rows, sl, (_rms(o, normw_ref[...]) * (g * _sigmoid(g))).astype(BF16))

    for ci in range(n_chunks):
        gates(ci)
        yield
        partial = [head_scores(ci, h) for h in range(HG_HEADS)]
        yield
        for parts in partial:
            head_output(ci, *parts)
        yield


def _hgrn2_ffn_kernel(proj_ref, lbl_ref, hnw_ref, yfox_ref, h_ref,
                      woa_ref, wob_ref, nw_ref, wg_ref, wu_ref, wd_ref,
                      out_ref, state_t, y_scr, act_ref, *, layer, n_tiles, tiles_per_seq):
    i = pl.program_id(0)
    tile = jnp.minimum(i, n_tiles - 1)
    slot = i % 2

    @pl.when(i == 0)
    def _():
        y_scr[...] = jnp.zeros_like(y_scr)

    @pl.when(tile % tiles_per_seq == 0)
    def _():
        state_t[...] = jnp.zeros_like(state_t)

    ya = y_scr[1 - slot]

    def store_y(rows, lanes, value):
        y_scr[slot, rows, lanes] = value

    stages = _hgrn2_block(proj_ref, lbl_ref, hnw_ref, store_y, state_t, layer)
    _ffn_tile(ya, yfox_ref[...], h_ref[...], woa_ref, wob_ref, nw_ref, wg_ref, wu_ref, wd_ref,
              act_ref, out_ref, stages)


def _hgrn2_ffn(proj, lb_logits, hg_norm_w, y_fox, h2d, w_out, j, ffn_norm_w, w_gate, w_up, w_down,
               layer):
    b, t_len, _ = proj.shape
    width = HG_HEADS * HG_DIM
    tiles_per_seq = t_len // ROW_TILE
    n_tiles = b * tiles_per_seq
    mixer3, _, _, ffn = _tile_maps(n_tiles, tiles_per_seq)
    kern = functools.partial(_hgrn2_ffn_kernel, layer=j, n_tiles=n_tiles,
                             tiles_per_seq=tiles_per_seq)
    return pl.pallas_call(
        kern,
        grid=(n_tiles + 1,),
        in_specs=[
            pl.BlockSpec((1, ROW_TILE, 4 * width), mixer3),
            _const_spec(lb_logits.shape),
            _const_spec(hg_norm_w.shape),
            pl.BlockSpec((ROW_TILE, width), ffn),
            pl.BlockSpec((ROW_TILE, D_MODEL), ffn),
        ] + _ffn_weight_specs(width, j, layer),
        out_specs=pl.BlockSpec((ROW_TILE, D_MODEL), ffn),
        out_shape=jax.ShapeDtypeStruct(h2d.shape, F32),
        scratch_shapes=[
            pltpu.VMEM((HG_HEADS, HG_DIM, HG_DIM), F32),
            pltpu.VMEM((2, ROW_TILE, width), BF16),
            pltpu.VMEM((ROW_TILE, FFN_HIDDEN), BF16),
        ],
        compiler_params=_params(("arbitrary",)),
        name="hgrn2_ffn",
    )(proj, lb_logits, hg_norm_w, y_fox, h2d, w_out, w_out, ffn_norm_w, w_gate, w_up, w_down)


def _forget_bias_kernel(gate_ref, bias_ref, out_ref):
    t_len = gate_ref.shape[1]
    blk = V7X_LANES
    row_i = lax.broadcasted_iota(jnp.int32, (blk, blk), 0)
    col_j = lax.broadcasted_iota(jnp.int32, (blk, blk), 1)
    tri = jnp.where(row_i >= col_j, 1.0, 0.0).astype(BF16)
    lane = lax.broadcasted_iota(jnp.int32, (1, V7X_LANES), 1)
    offset = jnp.zeros((1, V7X_LANES), F32)
    for i in range(t_len // blk):
        x = gate_ref[0, i * blk:(i + 1) * blk, :] + bias_ref[...]
        log_f = jnp.minimum(x, 0.0) - jnp.log1p(jnp.exp(-jnp.abs(x)))
        local = _tri_dot(tri, log_f)
        hi, mid, lo = _split3((local + offset) * -LOG2_E)
        offset = offset + local[blk - 1:blk, :]
        hi, mid, lo = hi.astype(F32), mid.astype(F32), lo.astype(F32)
        for h in range(FOX_HEADS):
            parts = jnp.where(lane == 0, hi[:, h:h + 1],
                              jnp.where(lane == 1, mid[:, h:h + 1],
                                        jnp.where(lane == 2, lo[:, h:h + 1], 0.0)))
            out_ref[0, i * blk:(i + 1) * blk, h * V7X_LANES:(h + 1) * V7X_LANES] = parts.astype(BF16)


def _forget_bias(gates, bias):
    b, t_len, _ = gates.shape
    return pl.pallas_call(
        _forget_bias_kernel,
        grid=(b,),
        in_specs=[pl.BlockSpec((1, t_len, GATE_PAD), lambda i: (i, 0, 0)), _const_spec(bias.shape)],
        out_specs=pl.BlockSpec((1, t_len, FOX_HEADS * V7X_LANES), lambda i: (i, 0, 0)),
        out_shape=jax.ShapeDtypeStruct((b, t_len, FOX_HEADS * V7X_LANES), BF16),
        compiler_params=_params(("parallel",)),
        name="forget_bias",
    )(gates, bias)


def _fox_kernel(q_ref, k_ref, kb_ref, v_ref, y_ref):
    sub, keys = ATT_SUB, ATT_KEYS
    n_sub = ATT_ROWS // sub
    qi = pl.program_id(2)
    q0 = pl.multiple_of(qi * ATT_ROWS, ATT_ROWS)
    lane = lax.broadcasted_iota(jnp.int32, (sub, V7X_LANES), 1)
    q_ones = jnp.where(lane < 3, 1.0, 0.0).astype(BF16)

    def tile(state, r, k, v, mask):
        m, acc = state
        q = jnp.concatenate([q_ref[0, r * sub:(r + 1) * sub, :], q_ones], axis=-1)
        s = _dot_nt(q, k)
        if mask is not None:
            s = jnp.where(mask, s, -jnp.inf)
        m_new = jnp.maximum(m, jnp.max(s, axis=-1, keepdims=True))
        alpha = jnp.exp2(m - m_new)
        p = jnp.exp2(s - m_new)
        acc = alpha * acc + _dot(p.astype(BF16), v)
        return m_new, acc

    def load(k0, width):
        k = jnp.concatenate([k_ref[0, pl.ds(k0, width), :], kb_ref[0, pl.ds(k0, width), :]], axis=-1)
        v = jnp.concatenate([v_ref[0, pl.ds(k0, width), :], jnp.ones((width, V7X_LANES), BF16)],
                            axis=-1)
        return k, v

    def before(kj, states):
        k, v = load(pl.multiple_of(kj * keys, keys), keys)
        return tuple(tile(states[r], r, k, v, None) for r in range(n_sub))

    init = tuple((jnp.full((sub, 1), -jnp.inf, F32), jnp.zeros((sub, 2 * FOX_DIM), F32))
                 for _ in range(n_sub))
    states = list(lax.fori_loop(0, qi * (ATT_ROWS // keys), before, init))

    row_i = lax.broadcasted_iota(jnp.int32, (sub, sub), 0)
    col_j = lax.broadcasted_iota(jnp.int32, (sub, sub), 1)
    causal = row_i >= col_j
    for c in range(n_sub):
        k, v = load(pl.multiple_of(q0 + c * sub, sub), sub)
        for r in range(c, n_sub):
            states[r] = tile(states[r], r, k, v, causal if r == c else None)
    for r in range(n_sub):
        _, acc = states[r]
        y_ref[0, r * sub:(r + 1) * sub, :] = (acc[:, 0:FOX_DIM] / acc[:, FOX_DIM:]).astype(y_ref.dtype)


def _fox(proj, key_bias):
    b, t_len, _ = proj.shape
    blk = ATT_ROWS
    qcol, kcol, vcol = (2048 // FOX_DIM, 2560 // FOX_DIM, 3072 // FOX_DIM)
    return pl.pallas_call(
        _fox_kernel,
        grid=(b, FOX_HEADS, t_len // blk),
        in_specs=[
            pl.BlockSpec((1, blk, FOX_DIM), lambda i, h, j: (i, j, qcol + h)),
            pl.BlockSpec((1, t_len, FOX_DIM), lambda i, h, j: (i, 0, kcol + h)),
            pl.BlockSpec((1, t_len, V7X_LANES), lambda i, h, j: (i, 0, h)),
            pl.BlockSpec((1, t_len, FOX_DIM), lambda i, h, j: (i, 0, vcol + h)),
        ],
        out_specs=pl.BlockSpec((1, blk, FOX_DIM), lambda i, h, j: (i, j, h)),
        out_shape=jax.ShapeDtypeStruct((b, t_len, FOX_HEADS * FOX_DIM), BF16),
        compiler_params=_params(("parallel", "parallel", "arbitrary")),
        name="fox_attention",
    )(proj, proj, key_bias, proj)


def _rotary_perm():
    width = RET_HEADS * RET_DK
    perm = np.zeros((2 * width,), np.int32)
    for base in (0, width):
        for half in range(2):
            for h in range(RET_HEADS):
                for i in range(RET_DK // 2):
                    perm[base + half * 128 + h * 32 + i] = base + h * RET_DK + 2 * i + half
    return perm


def _pad_lanes(a, width=GATE_PAD):
    return jnp.pad(a, [(0, 0)] * (a.ndim - 1) + [(0, width - a.shape[-1])])


def kernel(x, norm_mix, norm_ffn, ffn_w_gate, ffn_w_up, ffn_w_down, ab_w_in, ab_w_out, ret_gn_w,
           ssd_conv_w, ssd_conv_b, ssd_dt_bias, ssd_a_log, ssd_d, ssd_norm_w, cd_w_in, cd_w_out,
           hg_lb_logits, hg_norm_w, fox_f_bias, fox_q_norm_w, fox_k_norm_w):
    b, t_len, d = x.shape
    depth = norm_mix.shape[0]
    n_rows = b * t_len
    perm = _rotary_perm()

    half = RET_DK // 2
    freqs = ROPE_BASE ** (-jnp.linspace(0.0, 1.0, half, dtype=F32))
    ang = jnp.arange(t_len, dtype=F32)[:, None] * freqs[None, :]
    cos = jnp.tile(jnp.cos(ang), (1, RET_HEADS))
    sin = jnp.tile(jnp.sin(ang), (1, RET_HEADS))

    ab_w_out_b, cd_w_out_b = ab_w_out.astype(BF16), cd_w_out.astype(BF16)
    w_gate_b, w_up_b, w_down_b = (ffn_w_gate.astype(BF16), ffn_w_up.astype(BF16),
                                  ffn_w_down.astype(BF16))

    h2d = x.reshape(n_rows, d)
    for layer in range(depth):
        j = layer // 2
        if layer % 2 == 0:
            w_qk = ab_w_in[j, :, 0:2 * RET_HEADS * RET_DK][:, perm]
            proj, gates = _inproj(h2d, norm_mix[layer][None, :], ab_w_in, j, AB_MAIN, w_qk=w_qk)
            gates3 = gates.reshape(b, t_len, GATE_PAD)
            gates_t = jnp.transpose(gates3[:, :, 0:8], (0, 2, 1))
            h2d = _even_mixer_ffn(
                proj.reshape(b, t_len, AB_MAIN), gates3, gates_t, cos, sin,
                ret_gn_w[j].reshape(1, RET_HEADS * RET_DV),
                ssd_conv_w[j], ssd_conv_b[j][None, :],
                _pad_lanes(ssd_dt_bias[j][None, :]), ssd_dt_bias[j][:, None],
                _pad_lanes(ssd_a_log[j][None, :]), ssd_a_log[j][:, None],
                jnp.repeat(ssd_d[j], SSD_HEAD_DIM)[None, :], ssd_norm_w[j][None, :],
                h2d, ab_w_out_b, j, norm_ffn[layer][None, :], w_gate_b, w_up_b, w_down_b, layer)
        else:
            qk_w = jnp.concatenate([jnp.tile(fox_q_norm_w[j] * (FOX_DIM ** -0.5 * LOG2_E), FOX_HEADS),
                                    jnp.tile(fox_k_norm_w[j], FOX_HEADS)])[None, :]
            proj, gates = _inproj(h2d, norm_mix[layer][None, :], cd_w_in, j, CD_MAIN,
                                  qk_w=qk_w, norm_cols=(2048, 3072))
            proj3 = proj.reshape(b, t_len, CD_MAIN)
            key_bias = _forget_bias(gates.reshape(b, t_len, GATE_PAD),
                                    _pad_lanes(fox_f_bias[j][None, :]))
            y_fox = _fox(proj3, key_bias).reshape(n_rows, FOX_HEADS * FOX_DIM)
            h2d = _hgrn2_ffn(proj3, hg_lb_logits, hg_norm_w[j][None, :], y_fox, h2d, cd_w_out_b, j,
                             norm_ffn[layer][None, :], w_gate_b, w_up_b, w_down_b, layer)
    return h2d.reshape(b, t_len, d)
```

```python
import functools
import math

import numpy as np
import jax
import jax.numpy as jnp
from jax import lax
from jax.experimental import pallas as pl
from jax.experimental.pallas import tpu as pltpu

F32 = jnp.float32
BF16 = jnp.bfloat16

D_MODEL = 1024
EPS = 1e-6
ROPE_BASE = 10000.0
LOG2_E = 1.4426950408889634

RET_HEADS, RET_DK, RET_DV = 4, 64, 128
SSD_HEADS, SSD_HEAD_DIM, SSD_GROUPS, SSD_STATE, SSD_CONV = 8, 64, 2, 64, 4
SSD_INNER = SSD_HEADS * SSD_HEAD_DIM
SSD_CONV_DIM = SSD_INNER + 2 * SSD_GROUPS * SSD_STATE
HG_HEADS, HG_DIM = 4, 128
FOX_HEADS, FOX_DIM = 4, 128
FFN_HIDDEN = 2816

AB_MAIN = 2816
AB_GATES = SSD_HEADS
CD_MAIN = 3584
CD_GATES = FOX_HEADS
GATE_PAD = 128

V7X_LANES = 128
V7X_VMEM_LIMIT = 56 * 1024 * 1024

ROW_TILE = 512
COL_TILE = 512
FFN_TILE = 256
CHUNK = 128
SUB = 16
ATT_ROWS = 1024
ATT_SUB = 256
ATT_KEYS = 1024


def _dot(a, b):
    return jnp.dot(a, b, preferred_element_type=F32)


def _dot_nt(a, b):
    return lax.dot_general(a, b, (((1,), (1,)), ((), ())), preferred_element_type=F32)


def _dot_tn(a, b):
    return _dot(a.T.astype(BF16), b)


def _split3(x):
    hi = x.astype(BF16)
    r = x - hi.astype(F32)
    mid = r.astype(BF16)
    lo = (r - mid.astype(F32)).astype(BF16)
    return hi, mid, lo


def _tri_dot(tri, x):
    hi, mid, lo = _split3(x)
    return _dot(tri, hi) + _dot(tri, mid) + _dot(tri, lo)


def _dot_tri(x, tri):
    hi, mid, lo = _split3(x)
    return _dot(hi, tri) + _dot(mid, tri) + _dot(lo, tri)


def _sigmoid(x):
    return 1.0 / (1.0 + jnp.exp(-x))


def _softplus(x):
    return jnp.maximum(x, 0.0) + jnp.log1p(jnp.exp(-jnp.abs(x)))


def _rms(x, w):
    return x * lax.rsqrt(jnp.mean(x * x, axis=-1, keepdims=True) + EPS) * w


def _const_spec(shape):
    zeros = (0,) * len(shape)
    return pl.BlockSpec(shape, lambda *_: zeros, pipeline_mode=pl.Buffered(1))


def _params(semantics):
    return pltpu.CompilerParams(dimension_semantics=semantics,
                                vmem_limit_bytes=V7X_VMEM_LIMIT)


def _inproj_kernel(h_ref, nw_ref, w_ref, wqk_ref, qkw_ref, out_ref, gate_ref, wb_ref, *,
                   n_main, norm_cols):
    n_all = w_ref.shape[2]

    @pl.when(pl.program_id(0) == 0)
    def _():
        for c0 in range(0, n_main, COL_TILE):
            width = min(COL_TILE, n_main - c0)
            wb_ref[:, c0:c0 + width] = w_ref[0, :, c0:c0 + width].astype(BF16)
        wb_ref[:, n_main:n_main + GATE_PAD] = jnp.zeros((D_MODEL, GATE_PAD), BF16)
        wb_ref[:, n_main:n_all] = w_ref[0, :, n_main:n_all].astype(BF16)
        if wqk_ref is not None:
            wb_ref[:, 0:wqk_ref.shape[1]] = wqk_ref[...].astype(BF16)

    x = h_ref[...]
    ub = _rms(x, nw_ref[...]).astype(BF16)
    for c0 in range(0, n_main, COL_TILE):
        width = min(COL_TILE, n_main - c0)
        y = _dot(ub, wb_ref[:, c0:c0 + width])
        for l0 in range(0, width, V7X_LANES):
            col = c0 + l0
            piece = y[:, l0:l0 + V7X_LANES]
            if norm_cols[0] <= col < norm_cols[1]:
                piece = _rms(piece, qkw_ref[:, col - norm_cols[0]:col - norm_cols[0] + V7X_LANES])
            out_ref[:, col:col + V7X_LANES] = piece.astype(out_ref.dtype)
    gate_ref[...] = _dot(ub, wb_ref[:, n_main:n_main + GATE_PAD])


def _inproj(h2d, norm_w, w_all, layer, n_main, w_qk=None, qk_w=None, norm_cols=(0, 0)):
    n_rows = h2d.shape[0]
    n_all = w_all.shape[2]

    def body(h_ref, nw_ref, w_ref, *rest):
        rest = list(rest)
        wqk_ref = rest.pop(0) if w_qk is not None else None
        qkw_ref = rest.pop(0) if qk_w is not None else None
        _inproj_kernel(h_ref, nw_ref, w_ref, wqk_ref, qkw_ref, *rest,
                       n_main=n_main, norm_cols=norm_cols)

    operands = [h2d, norm_w, w_all]
    in_specs = [
        pl.BlockSpec((ROW_TILE, D_MODEL), lambda i: (i, 0)),
        _const_spec((1, D_MODEL)),
        pl.BlockSpec((1, D_MODEL, n_all), lambda i: (layer, 0, 0), pipeline_mode=pl.Buffered(1)),
    ]
    for extra in (w_qk, qk_w):
        if extra is not None:
            operands.append(extra)
            in_specs.append(_const_spec(extra.shape))
    return pl.pallas_call(
        body,
        grid=(n_rows // ROW_TILE,),
        in_specs=in_specs,
        out_specs=[
            pl.BlockSpec((ROW_TILE, n_main), lambda i: (i, 0)),
            pl.BlockSpec((ROW_TILE, GATE_PAD), lambda i: (i, 0)),
        ],
        out_shape=[
            jax.ShapeDtypeStruct((n_rows, n_main), BF16),
            jax.ShapeDtypeStruct((n_rows, GATE_PAD), F32),
        ],
        scratch_shapes=[pltpu.VMEM((D_MODEL, n_main + GATE_PAD), BF16)],
        compiler_params=_params(("arbitrary",)),
        name="inproj",
    )(*operands)


def _ffn_tile(ya, yb, h, woa_ref, wob_ref, nw_ref, wg_ref, wu_ref, wd_ref, act_ref, out_ref,
              between):
    between = iter(between)
    next(between, None)
    h1 = h + _dot(ya, woa_ref[0]) + _dot(yb, wob_ref[0])
    ub = _rms(h1, nw_ref[...]).astype(BF16)
    for c0 in range(0, FFN_HIDDEN, FFN_TILE):
        next(between, None)
        g = _dot(ub, wg_ref[0, :, c0:c0 + FFN_TILE])
        up = _dot(ub, wu_ref[0, :, c0:c0 + FFN_TILE])
        act_ref[:, c0:c0 + FFN_TILE] = (g * _sigmoid(g) * up).astype(BF16)
    for n0 in range(0, D_MODEL, FFN_TILE):
        next(between, None)
        out_ref[:, n0:n0 + FFN_TILE] = (
            h1[:, n0:n0 + FFN_TILE] + _dot(act_ref[...], wd_ref[0, :, n0:n0 + FFN_TILE]))
    for _ in between:
        pass


def _layer_spec(shape, layer, row_block=0):
    return pl.BlockSpec((1,) + shape, lambda i: (layer, row_block, 0), pipeline_mode=pl.Buffered(1))


def _even_tables(decay_tab, qdec_tab, kdec_tab, alast_tab):
    c = CHUNK
    row_i = lax.broadcasted_iota(jnp.int32, (c, c), 0)
    col_j = lax.broadcasted_iota(jnp.int32, (c, c), 1)
    lane = lax.broadcasted_iota(jnp.int32, (1, V7X_LANES), 1)
    log_gammas = [math.log1p(-(2.0 ** (-5 - h))) for h in range(RET_HEADS)]
    dij = (row_i - col_j).astype(F32)
    pos = lax.broadcasted_iota(jnp.int32, (c, V7X_LANES), 0).astype(F32)
    for h, log_g in enumerate(log_gammas):
        decay_tab[h] = jnp.exp(jnp.where(row_i >= col_j, dij * log_g, -jnp.inf))
        qdec_tab[h] = jnp.exp((pos + 1.0) * log_g)
    lane_log_g = jnp.zeros((1, V7X_LANES), F32)
    row_head = lax.broadcasted_iota(jnp.int32, (V7X_LANES, V7X_LANES), 0) // (RET_DK // 2)
    row_log_g = jnp.zeros((V7X_LANES, V7X_LANES), F32)
    for h, log_g in enumerate(log_gammas):
        lane_log_g = jnp.where(lane // (RET_DK // 2) == h, log_g, lane_log_g)
        row_log_g = jnp.where(row_head == h, log_g, row_log_g)
    kdec_tab[...] = jnp.exp((float(c - 1) - pos) * lane_log_g) * (RET_DK ** -0.5)
    alast_tab[...] = jnp.exp(float(c) * row_log_g)


def _zip_stages(*generators):
    live = [iter(g) for g in generators]
    while live:
        still = []
        for g in live:
            try:
                next(g)
                still.append(g)
            except StopIteration:
                pass
        live = still
        if live:
            yield


def _retention_stages(r0, proj_ref, cos_ref, sin_ref, gnw_ref, store_ret,
                      ret_state, decay_tab, qdec_tab, kdec_tab, alast_tab):
    c = CHUNK
    rows = slice(r0, r0 + c)
    lane = lax.broadcasted_iota(jnp.int32, (1, V7X_LANES), 1)

    cos = cos_ref[rows, :]
    sin = sin_ref[rows, :]
    q1 = proj_ref[0, rows, 0:128].astype(F32)
    q2 = proj_ref[0, rows, 128:256].astype(F32)
    k1 = proj_ref[0, rows, 256:384].astype(F32)
    k2 = proj_ref[0, rows, 384:512].astype(F32)
    rq1 = q1 * cos - q2 * sin
    rq2 = q1 * sin + q2 * cos
    k_scale = RET_DK ** -0.5
    rk1 = k1 * cos - k2 * sin
    rk2 = k1 * sin + k2 * cos
    yield
    k_t = (jnp.concatenate([rk1.T, rk2.T], axis=0) * k_scale).astype(BF16)
    lane_head = lane // (RET_DK // 2)
    kdec = kdec_tab[...]
    k_out1 = (rk1 * kdec).T.astype(BF16)
    k_out2 = (rk2 * kdec).T.astype(BF16)
    qms = []
    for h in range(RET_HEADS):
        hm = lane_head == h
        qms.append(jnp.concatenate([jnp.where(hm, rq1, 0.0), jnp.where(hm, rq2, 0.0)],
                                   axis=-1).astype(BF16))
    yield
    scores = [_dot(qms[h], k_t) for h in range(RET_HEADS)]
    inters = [_dot(qms[h], ret_state[:, 128 * h:128 * h + 128].astype(BF16))
              for h in range(RET_HEADS)]
    v_all = proj_ref[0, rows, 512:1024]
    upd = jnp.concatenate([_dot(k_out1, v_all), _dot(k_out2, v_all)], axis=0)
    yield
    alast = alast_tab[...]
    for s0 in range(0, 2 * V7X_LANES, V7X_LANES):
        for c0 in range(0, RET_HEADS * RET_DV, V7X_LANES):
            ret_state[s0:s0 + V7X_LANES, c0:c0 + V7X_LANES] = (
                ret_state[s0:s0 + V7X_LANES, c0:c0 + V7X_LANES] * alast
                + upd[s0:s0 + V7X_LANES, c0:c0 + V7X_LANES])
    intras = [_dot((scores[h] * decay_tab[h]).astype(BF16),
                   proj_ref[0, rows, 512 + 128 * h:640 + 128 * h]) for h in range(RET_HEADS)]
    for h in range(RET_HEADS):
        o = intras[h] + inters[h] * qdec_tab[h]
        mu = jnp.mean(o, axis=-1, keepdims=True)
        d = o - mu
        var = jnp.mean(d * d, axis=-1, keepdims=True)
        on = d * lax.rsqrt(var + EPS) * gnw_ref[:, 128 * h:128 * h + 128]
        g = proj_ref[0, rows, 1024 + 128 * h:1152 + 128 * h].astype(F32)
        store_ret(rows, slice(128 * h, 128 * h + 128), (on * (g * _sigmoid(g))).astype(BF16))


def _ssd_stages(r0, proj_ref, gate_ref, gate_t_ref, convw_ref, convb_ref, dtb_ref, dtb_t_ref,
                alog_ref, alog_t_ref, dskip_ref, normw_ref, store_ssd,
                ssd_state, conv_buf, xbc_buf):
    c = CHUNK
    rows = slice(r0, r0 + c)
    row_i = lax.broadcasted_iota(jnp.int32, (c, c), 0)
    col_j = lax.broadcasted_iota(jnp.int32, (c, c), 1)
    causal = row_i >= col_j
    lane = lax.broadcasted_iota(jnp.int32, (1, V7X_LANES), 1)

    conv_buf[8:8 + c, :] = proj_ref[0, rows, 2048:2816].astype(F32)
    acc = jnp.zeros((c, SSD_CONV_DIM), F32) + convb_ref[...]
    for k in range(SSD_CONV):
        off = 8 - (SSD_CONV - 1) + k
        acc = acc + convw_ref[k:k + 1, :] * conv_buf[off:off + c, :]
    conv_buf[0:8, :] = conv_buf[c:c + 8, :]
    xbc_buf[...] = acc * _sigmoid(acc)
    xs = xbc_buf[:, 0:SSD_INNER]
    bm = xbc_buf[:, 512:640]
    cm = xbc_buf[:, 640:768]

    dt = _softplus(gate_ref[0, rows, :] + dtb_ref[...])
    la = -dt * jnp.exp(alog_ref[...])
    dt_t = _softplus(gate_t_ref[0, :, rows] + dtb_t_ref[...])
    la_t = -dt_t * jnp.exp(alog_t_ref[...])
    yield
    tri = jnp.where(causal, 1.0, 0.0).astype(BF16)
    tri_t = jnp.where(row_i <= col_j, 1.0, 0.0).astype(BF16)
    cum = _tri_dot(tri, la)
    cum_t = _dot_tri(la_t, tri_t)
    lane_half = lane // SSD_HEAD_DIM
    heads_per_group = SSD_HEADS // SSD_GROUPS
    cgs, bgs, gmats, xms = [], [], [], []
    for g in range(SSD_GROUPS):
        gm = lane_half == g
        cgs.append(jnp.where(gm, cm, 0.0))
        bgs.append(jnp.where(gm, bm, 0.0))
        gmats.append(_dot_nt(cgs[g].astype(BF16), bgs[g].astype(BF16)))
    for h in range(SSD_HEADS):
        pair, half = h // 2, h % 2
        xms.append(jnp.where(lane_half == half, xs[:, 128 * pair:128 * pair + 128], 0.0).astype(BF16))
    yield
    cum_last = cum[c - 1:c, :]
    q_scale = jnp.exp(cum)
    k_scale_ssd = dt * jnp.exp(cum_last - cum)
    s_scale = jnp.exp(cum_last)
    inters, updates, probs = [], [], []
    for h in range(SSD_HEADS):
        g = h // heads_per_group
        s_h = ssd_state[h]
        inters.append(_dot((cgs[g] * q_scale[:, h:h + 1]).astype(BF16), s_h.astype(BF16)))
        updates.append(s_h * s_scale[:, h:h + 1] + _dot_tn(bgs[g] * k_scale_ssd[:, h:h + 1], xms[h]))
        cc = cum[:, h:h + 1]
        cr = cum_t[h:h + 1, :]
        lmat = jnp.exp(jnp.where(causal, cc - cr, -jnp.inf)) * dt_t[h:h + 1, :]
        probs.append((gmats[g] * lmat).astype(BF16))
    yield
    intras = [_dot(probs[h], xms[h]) for h in range(SSD_HEADS)]
    for h in range(SSD_HEADS):
        ssd_state[h] = updates[h]
    y_pairs =[intras[2 * pr] + inters[2 * pr] + intras[2 * pr + 1] + inters[2 * pr + 1]
               for pr in range(SSD_HEADS // 2)]
    y = jnp.concatenate(y_pairs, axis=-1) + dskip_ref[...] * xbc_buf[:, 0:SSD_INNER]
    z = proj_ref[0, rows, 1536:2048].astype(F32)
    y = y * (z * _sigmoid(z))
    group = SSD_INNER // SSD_GROUPS
    for g in range(SSD_GROUPS):
        store_ssd(rows, slice(group * g, group * (g + 1)),
                  _rms(y[:, group * g:group * (g + 1)],
                       normw_ref[:, group * g:group * (g + 1)]).astype(BF16))


def _tile_maps(n_tiles, tiles_per_seq):
    def mixer3(i):
        tile = jnp.minimum(i, n_tiles - 1)
        return tile // tiles_per_seq, tile % tiles_per_seq, 0

    def mixer_t(i):
        tile = jnp.minimum(i, n_tiles - 1)
        return tile // tiles_per_seq, 0, tile % tiles_per_seq

    def mixer_pos(i):
        return jnp.minimum(i, n_tiles - 1) % tiles_per_seq, 0

    def ffn(i):
        return jnp.maximum(i - 1, 0), 0

    return mixer3, mixer_t, mixer_pos, ffn


def _ffn_weight_specs(half, j, layer):
    return [
        _layer_spec((half, D_MODEL), j, 0),
        _layer_spec((half, D_MODEL), j, 1),
        _const_spec((1, D_MODEL)),
        _layer_spec((D_MODEL, FFN_HIDDEN), layer),
        _layer_spec((D_MODEL, FFN_HIDDEN), layer),
        _layer_spec((FFN_HIDDEN, D_MODEL), layer),
    ]


def _even_ffn_kernel(proj_ref, gate_ref, gate_t_ref, cos_ref, sin_ref, gnw_ref,
                     convw_ref, convb_ref, dtb_ref, dtb_t_ref, alog_ref, alog_t_ref,
                     dskip_ref, normw_ref,
                     h_ref, woa_ref, wob_ref, nw_ref, wg_ref, wu_ref, wd_ref,
                     out_ref,
                     ret_state, ssd_state, conv_buf, xbc_buf, decay_tab, qdec_tab, kdec_tab,
                     alast_tab, yret_scr, yssd_scr, act_ref, *, n_tiles, tiles_per_seq):
    i = pl.program_id(0)
    tile = jnp.minimum(i, n_tiles - 1)
    slot = i % 2

    @pl.when(i == 0)
    def _():
        _even_tables(decay_tab, qdec_tab, kdec_tab, alast_tab)
        yret_scr[...] = jnp.zeros_like(yret_scr)
        yssd_scr[...] = jnp.zeros_like(yssd_scr)

    @pl.when(tile % tiles_per_seq == 0)
    def _():
        ret_state[...] = jnp.zeros_like(ret_state)
        ssd_state[...] = jnp.zeros_like(ssd_state)
        conv_buf[0:8, :] = jnp.zeros((8, SSD_CONV_DIM), F32)

    ya = yret_scr[1 - slot]
    yb = yssd_scr[1 - slot]

    def store_ret(rows, lanes, value):
        yret_scr[slot, rows, lanes] = value

    def store_ssd(rows, lanes, value):
        yssd_scr[slot, rows, lanes] = value

    def stages():
        for ci in range(ROW_TILE // CHUNK):
            yield from _zip_stages(
                _retention_stages(ci * CHUNK, proj_ref, cos_ref, sin_ref, gnw_ref, store_ret,
                                  ret_state, decay_tab, qdec_tab, kdec_tab, alast_tab),
                _ssd_stages(ci * CHUNK, proj_ref, gate_ref, gate_t_ref, convw_ref, convb_ref,
                            dtb_ref, dtb_t_ref, alog_ref, alog_t_ref, dskip_ref, normw_ref,
                            store_ssd, ssd_state, conv_buf, xbc_buf))
            yield

    _ffn_tile(ya, yb, h_ref[...], woa_ref, wob_ref, nw_ref, wg_ref, wu_ref, wd_ref,
              act_ref, out_ref, stages())


def _even_mixer_ffn(proj, gates, gates_t, cos, sin, gn_w, conv_w, conv_b, dt_b, dt_b_t,
                    a_log, a_log_t, d_skip, norm_w, h2d, w_out, j, ffn_norm_w, w_gate, w_up, w_down,
                    layer):
    b, t_len, _ = proj.shape
    c = CHUNK
    half = RET_HEADS * RET_DV
    tiles_per_seq = t_len // ROW_TILE
    n_tiles = b * tiles_per_seq
    mixer3, mixer_t, mixer_pos, ffn = _tile_maps(n_tiles, tiles_per_seq)
    kern = functools.partial(_even_ffn_kernel, n_tiles=n_tiles, tiles_per_seq=tiles_per_seq)
    return pl.pallas_call(
        kern,
        grid=(n_tiles + 1,),
        in_specs=[
            pl.BlockSpec((1, ROW_TILE, AB_MAIN), mixer3),
            pl.BlockSpec((1, ROW_TILE, GATE_PAD), mixer3),
            pl.BlockSpec((1, 8, ROW_TILE), mixer_t),
            pl.BlockSpec((ROW_TILE, V7X_LANES), mixer_pos),
            pl.BlockSpec((ROW_TILE, V7X_LANES), mixer_pos),
            _const_spec(gn_w.shape), _const_spec(conv_w.shape), _const_spec(conv_b.shape),
            _const_spec(dt_b.shape), _const_spec(dt_b_t.shape),
            _const_spec(a_log.shape), _const_spec(a_log_t.shape),
            _const_spec(d_skip.shape), _const_spec(norm_w.shape),
            pl.BlockSpec((ROW_TILE, D_MODEL), ffn),
        ] + _ffn_weight_specs(half, j, layer),
        out_specs=pl.BlockSpec((ROW_TILE, D_MODEL), ffn),
        out_shape=jax.ShapeDtypeStruct(h2d.shape, F32),
        scratch_shapes=[
            pltpu.VMEM((2 * V7X_LANES, RET_HEADS * RET_DV), F32),
            pltpu.VMEM((SSD_HEADS, V7X_LANES, V7X_LANES), F32),
            pltpu.VMEM((c + 8, SSD_CONV_DIM), F32),
            pltpu.VMEM((c, SSD_CONV_DIM), F32),
            pltpu.VMEM((RET_HEADS, c, c), F32),
            pltpu.VMEM((RET_HEADS, c, V7X_LANES), F32),
            pltpu.VMEM((c, V7X_LANES), F32),
            pltpu.VMEM((V7X_LANES, V7X_LANES), F32),
            pltpu.VMEM((2, ROW_TILE, half), BF16),
            pltpu.VMEM((2, ROW_TILE, SSD_INNER), BF16),
            pltpu.VMEM((ROW_TILE, FFN_HIDDEN), BF16),
        ],
        compiler_params=_params(("arbitrary",)),
        name="even_mixer_ffn",
    )(proj, gates, gates_t, cos, sin, gn_w, conv_w, conv_b, dt_b, dt_b_t,
      a_log, a_log_t, d_skip, norm_w, h2d, w_out, w_out, ffn_norm_w, w_gate, w_up, w_down)


def _hgrn2_block(proj_ref, lbl_ref, normw_ref, store_y, state_t, layer):
    c = CHUNK
    n_chunks = ROW_TILE // c

    logits = lbl_ref[...]
    e = jnp.exp(logits - jnp.max(logits, axis=0, keepdims=True))
    prob = e / jnp.sum(e, axis=0, keepdims=True)
    lb = jnp.zeros((1, HG_HEADS * HG_DIM), F32)
    for i in range(1, layer + 1):
        lb = lb + prob[i:i + 1, :]

    row_i = lax.broadcasted_iota(jnp.int32, (c, c), 0)
    col_j = lax.broadcasted_iota(jnp.int32, (c, c), 1)
    tri = jnp.where(row_i >= col_j, 1.0, 0.0).astype(BF16)

    width = HG_HEADS * HG_DIM

    carried = {"states": [None] * HG_HEADS}

    def gates(ci):
        rows = slice(ci * c, (ci + 1) * c)
        zf = proj_ref[0, rows, width:2 * width].astype(F32)
        f = lb + (1.0 - lb) * _sigmoid(zf)
        carried["k_all"] = (1.0 - lb) * _sigmoid(-zf)
        carried["cum_all"] = _tri_dot(tri, jnp.log(f))

    def head_scores(ci, h):
        rows = slice(ci * c, (ci + 1) * c)
        k_all, cum_all = carried["k_all"], carried["cum_all"]
        sl = slice(HG_DIM * h, HG_DIM * (h + 1))
        q = proj_ref[0, rows, HG_DIM * h:HG_DIM * (h + 1)].astype(F32)
        k = k_all[:, sl]
        cum = cum_all[:, sl]
        v = proj_ref[0, rows, 2 * width + HG_DIM * h:2 * width + HG_DIM * (h + 1)]
        st = state_t[h] if ci == 0 else carried["states"][h]
        inter = _dot_nt((q * jnp.exp(cum)).astype(BF16), st.astype(BF16))
        n_blk = c // SUB
        ends = [cum[(j + 1) * SUB - 1:(j + 1) * SUB, :] for j in range(n_blk)]
        starts = [jnp.zeros((1, HG_DIM), F32)] + ends[:-1]
        start_rows = jnp.concatenate(
            [jnp.broadcast_to(s, (SUB, HG_DIM)) for s in starts], axis=0)
        end_rows = jnp.concatenate(
            [jnp.broadcast_to(e, (SUB, HG_DIM)) for e in ends], axis=0)
        q_blk = (q * jnp.exp(cum - start_rows)).astype(BF16)
        k_end = k * jnp.exp(end_rows - cum)
        k_end_b = k_end.astype(BF16)
        k_own = (k * jnp.exp(start_rows - cum)).astype(BF16)
        blocks = []
        for i in range(n_blk):
            pieces = []
            for j in range(i):
                if j == i - 1:
                    pieces.append(k_end_b[j * SUB:(j + 1) * SUB, :])
                else:
                    gap = jnp.exp(starts[i] - ends[j])
                    pieces.append((k_end[j * SUB:(j + 1) * SUB, :] * gap).astype(BF16))
            pieces.append(k_own[i * SUB:(i + 1) * SUB, :])
            if i + 1 < n_blk:
                pieces.append(jnp.zeros((c - (i + 1) * SUB, HG_DIM), BF16))
            kb = jnp.concatenate(pieces, axis=0)
            blocks.append(_dot_nt(q_blk[i * SUB:(i + 1) * SUB, :], kb))
        scores = jnp.where(row_i >= col_j, jnp.concatenate(blocks, axis=0), 0.0).astype(BF16)
        cl = cum[c - 1:c, :]
        k_out = (k * jnp.exp(cl - cum)).astype(BF16)
        return rows, sl, h, inter, scores, v, st * jnp.exp(cl), k_out

    def head_output(ci, rows, sl, h, inter, scores, v, st_decayed, k_out):
        o = inter + _dot(scores, v)
        new_state = st_decayed + _dot_tn(v.astype(F32), k_out)
        if ci == n_chunks - 1:
            state_t[h] = new_state
        else:
            carried["states"][h] = new_state
        g = proj_ref[0, rows, 3 * width + HG_DIM * h:3 * width + HG_DIM * (h + 1)].astype(F32)
        store_y(rows, sl, (_rms(o, normw_ref[...]) * (g * _sigmoid(g))).astype(BF16))

    for ci in range(n_chunks):
        gates(ci)
        yield
        partial = [head_scores(ci, h) for h in range(HG_HEADS)]
        yield
        for parts in partial:
            head_output(ci, *parts)
        yield


def _hgrn2_ffn_kernel(proj_ref, lbl_ref, hnw_ref, yfox_ref, h_ref,
                      woa_ref, wob_ref, nw_ref, wg_ref, wu_ref, wd_ref,
                      out_ref, state_t, y_scr, act_ref, *, layer, n_tiles, tiles_per_seq):
    i = pl.program_id(0)
    tile = jnp.minimum(i, n_tiles - 1)
    slot = i % 2

    @pl.when(i == 0)
    def _():
        y_scr[...] = jnp.zeros_like(y_scr)

    @pl.when(tile % tiles_per_seq == 0)
    def _():
        state_t[...] = jnp.zeros_like(state_t)

    ya = y_scr[1 - slot]

    def store_y(rows, lanes, value):
        y_scr[slot, rows, lanes] = value

    stages = _hgrn2_block(proj_ref, lbl_ref, hnw_ref, store_y, state_t, layer)
    _ffn_tile(ya, yfox_ref[...], h_ref[...], woa_ref, wob_ref, nw_ref, wg_ref, wu_ref, wd_ref,
              act_ref, out_ref, stages)


def _hgrn2_ffn(proj, lb_logits, hg_norm_w, y_fox, h2d, w_out, j, ffn_norm_w, w_gate, w_up, w_down,
               layer):
    b, t_len, _ = proj.shape
    width = HG_HEADS * HG_DIM
    tiles_per_seq = t_len // ROW_TILE
    n_tiles = b * tiles_per_seq
    mixer3, _, _, ffn = _tile_maps(n_tiles, tiles_per_seq)
    kern = functools.partial(_hgrn2_ffn_kernel, layer=j, n_tiles=n_tiles,
                             tiles_per_seq=tiles_per_seq)
    return pl.pallas_call(
        kern,
        grid=(n_tiles + 1,),
        in_specs=[
            pl.BlockSpec((1, ROW_TILE, 4 * width), mixer3),
            _const_spec(lb_logits.shape),
            _const_spec(hg_norm_w.shape),
            pl.BlockSpec((ROW_TILE, width), ffn),
            pl.BlockSpec((ROW_TILE, D_MODEL), ffn),
        ] + _ffn_weight_specs(width, j, layer),
        out_specs=pl.BlockSpec((ROW_TILE, D_MODEL), ffn),
        out_shape=jax.ShapeDtypeStruct(h2d.shape, F32),
        scratch_shapes=[
            pltpu.VMEM((HG_HEADS, HG_DIM, HG_DIM), F32),
            pltpu.VMEM((2, ROW_TILE, width), BF16),
            pltpu.VMEM((ROW_TILE, FFN_HIDDEN), BF16),
        ],
        compiler_params=_params(("arbitrary",)),
        name="hgrn2_ffn",
    )(proj, lb_logits, hg_norm_w, y_fox, h2d, w_out, w_out, ffn_norm_w, w_gate, w_up, w_down)


def _forget_bias_kernel(gate_ref, bias_ref, out_ref):
    t_len = gate_ref.shape[1]
    blk = V7X_LANES
    row_i = lax.broadcasted_iota(jnp.int32, (blk, blk), 0)
    col_j = lax.broadcasted_iota(jnp.int32, (blk, blk), 1)
    tri = jnp.where(row_i >= col_j, 1.0, 0.0).astype(BF16)
    lane = lax.broadcasted_iota(jnp.int32, (1, V7X_LANES), 1)
    offset = jnp.zeros((1, V7X_LANES), F32)
    for i in range(t_len // blk):
        x = gate_ref[0, i * blk:(i + 1) * blk, :] + bias_ref[...]
        log_f = jnp.minimum(x, 0.0) - jnp.log1p(jnp.exp(-jnp.abs(x)))
        local = _tri_dot(tri, log_f)
        hi, mid, lo = _split3((local + offset) * -LOG2_E)
        offset = offset + local[blk - 1:blk, :]
        hi, mid, lo = hi.astype(F32), mid.astype(F32), lo.astype(F32)
        for h in range(FOX_HEADS):
            parts = jnp.where(lane == 0, hi[:, h:h + 1],
                              jnp.where(lane == 1, mid[:, h:h + 1],
                                        jnp.where(lane == 2, lo[:, h:h + 1], 0.0)))
            out_ref[0, i * blk:(i + 1) * blk, h * V7X_LANES:(h + 1) * V7X_LANES] = parts.astype(BF16)


def _forget_bias(gates, bias):
    b, t_len, _ = gates.shape
    return pl.pallas_call(
        _forget_bias_kernel,
        grid=(b,),
        in_specs=[pl.BlockSpec((1, t_len, GATE_PAD), lambda i: (i, 0, 0)), _const_spec(bias.shape)],
        out_specs=pl.BlockSpec((1, t_len, FOX_HEADS * V7X_LANES), lambda i: (i, 0, 0)),
        out_shape=jax.ShapeDtypeStruct((b, t_len, FOX_HEADS * V7X_LANES), BF16),
        compiler_params=_params(("parallel",)),
        name="forget_bias",
    )(gates, bias)


def _fox_kernel(q_ref, k_ref, kb_ref, v_ref, y_ref):
    sub, keys = ATT_SUB, ATT_KEYS
    n_sub = ATT_ROWS // sub
    qi = pl.program_id(2)
    q0 = pl.multiple_of(qi * ATT_ROWS, ATT_ROWS)
    lane = lax.broadcasted_iota(jnp.int32, (sub, V7X_LANES), 1)
    q_ones = jnp.where(lane < 3, 1.0, 0.0).astype(BF16)

    def scores(r, k):
        q = jnp.concatenate([q_ref[0, r * sub:(r + 1) * sub, :], q_ones], axis=-1)
        return _dot_nt(q, k)

    def update(state, s, v, mask):
        m, acc = state
        if mask is not None:
            s = jnp.where(mask, s, -jnp.inf)
        m_new = jnp.maximum(m, jnp.max(s, axis=-1, keepdims=True))
        alpha = jnp.exp2(m - m_new)
        p = jnp.exp2(s - m_new)
        acc = alpha * acc + _dot(p.astype(BF16), v)
        return m_new, acc

    def tiles(states, pairs, k, v):
        ss = [scores(r, k) for r, _ in pairs]
        for (r, mask), s in zip(pairs, ss):
            states[r] = update(states[r], s, v, mask)
        return states

    def load(k0, width):
        k = jnp.concatenate([k_ref[0, pl.ds(k0, width), :], kb_ref[0, pl.ds(k0, width), :]], axis=-1)
        v = jnp.concatenate([v_ref[0, pl.ds(k0, width), :], jnp.ones((width, V7X_LANES), BF16)],
                            axis=-1)
        return k, v

    def before(kj, states):
        k, v = load(pl.multiple_of(kj * keys, keys), keys)
        return tuple(tiles(list(states), [(r, None) for r in range(n_sub)], k, v))

    init = tuple((jnp.full((sub, 1), -jnp.inf, F32), jnp.zeros((sub, 2 * FOX_DIM), F32))
                 for _ in range(n_sub))
    states = list(lax.fori_loop(0, qi * (ATT_ROWS // keys), before, init))

    row_i = lax.broadcasted_iota(jnp.int32, (sub, sub), 0)
    col_j = lax.broadcasted_iota(jnp.int32, (sub, sub), 1)
    causal = row_i >= col_j
    blocks = [load(pl.multiple_of(q0 + c * sub, sub), sub) for c in range(n_sub)]
    diag_scores = [[scores(r, blocks[c][0]) for r in range(c, n_sub)] for c in range(n_sub)]
    for c in range(n_sub):
        for r in range(c, n_sub):
            states[r] = update(states[r], diag_scores[c][r - c], blocks[c][1],
                               causal if r == c else None)
    for r in range(n_sub):
        _, acc = states[r]
        y_ref[0, r * sub:(r + 1) * sub, :] = (acc[:, 0:FOX_DIM] / acc[:, FOX_DIM:]).astype(y_ref.dtype)


def _fox(proj, key_bias):
    b, t_len, _ = proj.shape
    blk = ATT_ROWS
    qcol, kcol, vcol = (2048 // FOX_DIM, 2560 // FOX_DIM, 3072 // FOX_DIM)
    return pl.pallas_call(
        _fox_kernel,
        grid=(b, FOX_HEADS, t_len // blk),
        in_specs=[
            pl.BlockSpec((1, blk, FOX_DIM), lambda i, h, j: (i, j, qcol + h)),
            pl.BlockSpec((1, t_len, FOX_DIM), lambda i, h, j: (i, 0, kcol + h)),
            pl.BlockSpec((1, t_len, V7X_LANES), lambda i, h, j: (i, 0, h)),
            pl.BlockSpec((1, t_len, FOX_DIM), lambda i, h, j: (i, 0, vcol + h)),
        ],
        out_specs=pl.BlockSpec((1, blk, FOX_DIM), lambda i, h, j: (i, j, h)),
        out_shape=jax.ShapeDtypeStruct((b, t_len, FOX_HEADS * FOX_DIM), BF16),
        compiler_params=_params(("parallel", "parallel", "arbitrary")),
        name="fox_attention",
    )(proj, proj, key_bias, proj)


def _rotary_perm():
    width = RET_HEADS * RET_DK
    perm = np.zeros((2 * width,), np.int32)
    for base in (0, width):
        for half in range(2):
            for h in range(RET_HEADS):
                for i in range(RET_DK // 2):
                    perm[base + half * 128 + h * 32 + i] = base + h * RET_DK + 2 * i + half
    return perm


def _pad_lanes(a, width=GATE_PAD):
    return jnp.pad(a, [(0, 0)] * (a.ndim - 1) + [(0, width - a.shape[-1])])


def kernel(x, norm_mix, norm_ffn, ffn_w_gate, ffn_w_up, ffn_w_down, ab_w_in, ab_w_out, ret_gn_w,
           ssd_conv_w, ssd_conv_b, ssd_dt_bias, ssd_a_log, ssd_d, ssd_norm_w, cd_w_in, cd_w_out,
           hg_lb_logits, hg_norm_w, fox_f_bias, fox_q_norm_w, fox_k_norm_w):
    b, t_len, d = x.shape
    depth = norm_mix.shape[0]
    n_rows = b * t_len
    perm = _rotary_perm()

    half = RET_DK // 2
    freqs = ROPE_BASE ** (-jnp.linspace(0.0, 1.0, half, dtype=F32))
    ang = jnp.arange(t_len, dtype=F32)[:, None] * freqs[None, :]
    cos = jnp.tile(jnp.cos(ang), (1, RET_HEADS))
    sin = jnp.tile(jnp.sin(ang), (1, RET_HEADS))

    ab_w_out_b, cd_w_out_b = ab_w_out.astype(BF16), cd_w_out.astype(BF16)
    w_gate_b, w_up_b, w_down_b = (ffn_w_gate.astype(BF16), ffn_w_up.astype(BF16),
                                  ffn_w_down.astype(BF16))

    h2d = x.reshape(n_rows, d)
    for layer in range(depth):
        j = layer // 2
        if layer % 2 == 0:
            w_qk = ab_w_in[j, :, 0:2 * RET_HEADS * RET_DK][:, perm]
            proj, gates = _inproj(h2d, norm_mix[layer][None, :], ab_w_in, j, AB_MAIN, w_qk=w_qk)
            gates3 = gates.reshape(b, t_len, GATE_PAD)
            gates_t = jnp.transpose(gates3[:, :, 0:8], (0, 2, 1))
            h2d = _even_mixer_ffn(
                proj.reshape(b, t_len, AB_MAIN), gates3, gates_t, cos, sin,
                ret_gn_w[j].reshape(1, RET_HEADS * RET_DV),
                ssd_conv_w[j], ssd_conv_b[j][None, :],
                _pad_lanes(ssd_dt_bias[j][None, :]), ssd_dt_bias[j][:, None],
                _pad_lanes(ssd_a_log[j][None, :]), ssd_a_log[j][:, None],
                jnp.repeat(ssd_d[j], SSD_HEAD_DIM)[None, :], ssd_norm_w[j][None, :],
                h2d, ab_w_out_b, j, norm_ffn[layer][None, :], w_gate_b, w_up_b, w_down_b, layer)
        else:
            qk_w = jnp.concatenate([jnp.tile(fox_q_norm_w[j] * (FOX_DIM ** -0.5 * LOG2_E), FOX_HEADS),
                                    jnp.tile(fox_k_norm_w[j], FOX_HEADS)])[None, :]
            proj, gates = _inproj(h2d, norm_mix[layer][None, :], cd_w_in, j, CD_MAIN,
                                  qk_w=qk_w, norm_cols=(2048, 3072))
            proj3 = proj.reshape(b, t_len, CD_MAIN)
            key_bias = _forget_bias(gates.reshape(b, t_len, GATE_PAD),
                                    _pad_lanes(fox_f_bias[j][None, :]))
            y_fox = _fox(proj3, key_bias).reshape(n_rows, FOX_HEADS * FOX_DIM)
            h2d = _hgrn2_ffn(proj3, hg_lb_logits, hg_norm_w[j][None, :], y_fox, h2d, cd_w_out_b, j,
                             norm_ffn[layer][None, :], w_gate_b, w_up_b, w_down_b, layer)
    return h2d.reshape(b, t_len, d)
```

```python
import functools
import math

import numpy as np
import jax
import jax.numpy as jnp
from jax import lax
from jax.experimental import pallas as pl
from jax.experimental.pallas import tpu as pltpu

F32 = jnp.float32
BF16 = jnp.bfloat16

D_MODEL = 1024
EPS = 1e-6
ROPE_BASE = 10000.0
LOG2_E = 1.4426950408889634

RET_HEADS, RET_DK, RET_DV = 4, 64, 128
SSD_HEADS, SSD_HEAD_DIM, SSD_GROUPS, SSD_STATE, SSD_CONV = 8, 64, 2, 64, 4
SSD_INNER = SSD_HEADS * SSD_HEAD_DIM
SSD_CONV_DIM = SSD_INNER + 2 * SSD_GROUPS * SSD_STATE
HG_HEADS, HG_DIM = 4, 128
FOX_HEADS, FOX_DIM = 4, 128
FFN_HIDDEN = 2816

AB_MAIN = 2816
AB_GATES = SSD_HEADS
CD_MAIN = 3584
CD_GATES = FOX_HEADS
GATE_PAD = 128

V7X_LANES = 128
V7X_VMEM_LIMIT = 56 * 1024 * 1024

ROW_TILE = 512
COL_TILE = 512
FFN_TILE = 256
CHUNK = 128
SUB = 16
ATT_ROWS = 1024
ATT_SUB = 256
ATT_KEYS = 1024


def _dot(a, b):
    return jnp.dot(a, b, preferred_element_type=F32)


def _dot_nt(a, b):
    return lax.dot_general(a, b, (((1,), (1,)), ((), ())), preferred_element_type=F32)


def _dot_tn(a, b):
    return _dot(a.T.astype(BF16), b)


def _split3(x):
    hi = x.astype(BF16)
    r = x - hi.astype(F32)
    mid = r.astype(BF16)
    lo = (r - mid.astype(F32)).astype(BF16)
    return hi, mid, lo


def _tri_dot(tri, x):
    hi, mid, lo = _split3(x)
    return _dot(tri, hi) + _dot(tri, mid) + _dot(tri, lo)


def _dot_tri(x, tri):
    hi, mid, lo = _split3(x)
    return _dot(hi, tri) + _dot(mid, tri) + _dot(lo, tri)


def _sigmoid(x):
    return 1.0 / (1.0 + jnp.exp(-x))


def _softplus(x):
    return jnp.maximum(x, 0.0) + jnp.log1p(jnp.exp(-jnp.abs(x)))


def _rms(x, w):
    return x * lax.rsqrt(jnp.mean(x * x, axis=-1, keepdims=True) + EPS) * w


def _const_spec(shape):
    zeros = (0,) * len(shape)
    return pl.BlockSpec(shape, lambda *_: zeros, pipeline_mode=pl.Buffered(1))


def _params(semantics):
    return pltpu.CompilerParams(dimension_semantics=semantics,
                                vmem_limit_bytes=V7X_VMEM_LIMIT)


def _inproj_kernel(h_ref, nw_ref, w_ref, wqk_ref, qkw_ref, out_ref, gate_ref, wb_ref, *,
                   n_main, norm_cols):
    n_all = w_ref.shape[2]

    @pl.when(pl.program_id(0) == 0)
    def _():
        for c0 in range(0, n_main, COL_TILE):
            width = min(COL_TILE, n_main - c0)
            wb_ref[:, c0:c0 + width] = w_ref[0, :, c0:c0 + width].astype(BF16)
        wb_ref[:, n_main:n_main + GATE_PAD] = jnp.zeros((D_MODEL, GATE_PAD), BF16)
        wb_ref[:, n_main:n_all] = w_ref[0, :, n_main:n_all].astype(BF16)
        if wqk_ref is not None:
            wb_ref[:, 0:wqk_ref.shape[1]] = wqk_ref[...].astype(BF16)

    x = h_ref[...]
    ub = _rms(x, nw_ref[...]).astype(BF16)
    for c0 in range(0, n_main, COL_TILE):
        width = min(COL_TILE, n_main - c0)
        y = _dot(ub, wb_ref[:, c0:c0 + width])
        for l0 in range(0, width, V7X_LANES):
            col = c0 + l0
            piece = y[:, l0:l0 + V7X_LANES]
            if norm_cols[0] <= col < norm_cols[1]:
                piece = _rms(piece, qkw_ref[:, col - norm_cols[0]:col - norm_cols[0] + V7X_LANES])
            out_ref[:, col:col + V7X_LANES] = piece.astype(out_ref.dtype)
    gate_ref[...] = _dot(ub, wb_ref[:, n_main:n_main + GATE_PAD])


def _inproj(h2d, norm_w, w_all, layer, n_main, w_qk=None, qk_w=None, norm_cols=(0, 0)):
    n_rows = h2d.shape[0]
    n_all = w_all.shape[2]

    def body(h_ref, nw_ref, w_ref, *rest):
        rest = list(rest)
        wqk_ref = rest.pop(0) if w_qk is not None else None
        qkw_ref = rest.pop(0) if qk_w is not None else None
        _inproj_kernel(h_ref, nw_ref, w_ref, wqk_ref, qkw_ref, *rest,
                       n_main=n_main, norm_cols=norm_cols)

    operands = [h2d, norm_w, w_all]
    in_specs = [
        pl.BlockSpec((ROW_TILE, D_MODEL), lambda i: (i, 0)),
        _const_spec((1, D_MODEL)),
        pl.BlockSpec((1, D_MODEL, n_all), lambda i: (layer, 0, 0), pipeline_mode=pl.Buffered(1)),
    ]
    for extra in (w_qk, qk_w):
        if extra is not None:
            operands.append(extra)
            in_specs.append(_const_spec(extra.shape))
    return pl.pallas_call(
        body,
        grid=(n_rows // ROW_TILE,),
        in_specs=in_specs,
        out_specs=[
            pl.BlockSpec((ROW_TILE, n_main), lambda i: (i, 0)),
            pl.BlockSpec((ROW_TILE, GATE_PAD), lambda i: (i, 0)),
        ],
        out_shape=[
            jax.ShapeDtypeStruct((n_rows, n_main), BF16),
            jax.ShapeDtypeStruct((n_rows, GATE_PAD), F32),
        ],
        scratch_shapes=[pltpu.VMEM((D_MODEL, n_main + GATE_PAD), BF16)],
        compiler_params=_params(("arbitrary",)),
        name="inproj",
    )(*operands)


def _ffn_tile(ya, yb, h, woa_ref, wob_ref, nw_ref, wg_ref, wu_ref, wd_ref, act_ref, out_ref,
              between):
    between = iter(between)
    next(between, None)
    h1 = h + _dot(ya, woa_ref[0]) + _dot(yb, wob_ref[0])
    ub = _rms(h1, nw_ref[...]).astype(BF16)
    for c0 in range(0, FFN_HIDDEN, FFN_TILE):
        next(between, None)
        g = _dot(ub, wg_ref[0, :, c0:c0 + FFN_TILE])
        up = _dot(ub, wu_ref[0, :, c0:c0 + FFN_TILE])
        act_ref[:, c0:c0 + FFN_TILE] = (g * _sigmoid(g) * up).astype(BF16)
    for n0 in range(0, D_MODEL, FFN_TILE):
        next(between, None)
        out_ref[:, n0:n0 + FFN_TILE] = (
            h1[:, n0:n0 + FFN_TILE] + _dot(act_ref[...], wd_ref[0, :, n0:n0 + FFN_TILE]))
    for _ in between:
        pass


def _layer_spec(shape, layer, row_block=0):
    return pl.BlockSpec((1,) + shape, lambda i: (layer, row_block, 0), pipeline_mode=pl.Buffered(1))


def _even_tables(decay_tab, qdec_tab, kdec_tab, alast_tab):
    c = CHUNK
    row_i = lax.broadcasted_iota(jnp.int32, (c, c), 0)
    col_j = lax.broadcasted_iota(jnp.int32, (c, c), 1)
    lane = lax.broadcasted_iota(jnp.int32, (1, V7X_LANES), 1)
    log_gammas = [math.log1p(-(2.0 ** (-5 - h))) for h in range(RET_HEADS)]
    dij = (row_i - col_j).astype(F32)
    pos = lax.broadcasted_iota(jnp.int32, (c, V7X_LANES), 0).astype(F32)
    for h, log_g in enumerate(log_gammas):
        decay_tab[h] = jnp.exp(jnp.where(row_i >= col_j, dij * log_g, -jnp.inf))
        qdec_tab[h] = jnp.exp((pos + 1.0) * log_g)
    lane_log_g = jnp.zeros((1, V7X_LANES), F32)
    row_head = lax.broadcasted_iota(jnp.int32, (V7X_LANES, V7X_LANES), 0) // (RET_DK // 2)
    row_log_g = jnp.zeros((V7X_LANES, V7X_LANES), F32)
    for h, log_g in enumerate(log_gammas):
        lane_log_g = jnp.where(lane // (RET_DK // 2) == h, log_g, lane_log_g)
        row_log_g = jnp.where(row_head == h, log_g, row_log_g)
    kdec_tab[...] = jnp.exp((float(c - 1) - pos) * lane_log_g) * (RET_DK ** -0.5)
    alast_tab[...] = jnp.exp(float(c) * row_log_g)


def _zip_stages(*generators):
    live = [iter(g) for g in generators]
    while live:
        still = []
        for g in live:
            try:
                next(g)
                still.append(g)
            except StopIteration:
                pass
        live = still
        if live:
            yield


def _retention_stages(r0, proj_ref, cos_ref, sin_ref, gnw_ref, store_ret,
                      ret_state, decay_tab, qdec_tab, kdec_tab, alast_tab):
    c = CHUNK
    rows = slice(r0, r0 + c)
    lane = lax.broadcasted_iota(jnp.int32, (1, V7X_LANES), 1)

    cos = cos_ref[rows, :]
    sin = sin_ref[rows, :]
    q1 = proj_ref[0, rows, 0:128].astype(F32)
    q2 = proj_ref[0, rows, 128:256].astype(F32)
    k1 = proj_ref[0, rows, 256:384].astype(F32)
    k2 = proj_ref[0, rows, 384:512].astype(F32)
    rq1 = q1 * cos - q2 * sin
    rq2 = q1 * sin + q2 * cos
    k_scale = RET_DK ** -0.5
    rk1 = k1 * cos - k2 * sin
    rk2 = k1 * sin + k2 * cos
    yield
    k_t = (jnp.concatenate([rk1.T, rk2.T], axis=0) * k_scale).astype(BF16)
    lane_head = lane // (RET_DK // 2)
    kdec = kdec_tab[...]
    k_out1 = (rk1 * kdec).T.astype(BF16)
    k_out2 = (rk2 * kdec).T.astype(BF16)
    qms = []
    for h in range(RET_HEADS):
        hm = lane_head == h
        qms.append(jnp.concatenate([jnp.where(hm, rq1, 0.0), jnp.where(hm, rq2, 0.0)],
                                   axis=-1).astype(BF16))
    yield
    scores = [_dot(qms[h], k_t) for h in range(RET_HEADS)]
    inters = [_dot(qms[h], ret_state[:, 128 * h:128 * h + 128].astype(BF16))
              for h in range(RET_HEADS)]
    v_all = proj_ref[0, rows, 512:1024]
    upd = jnp.concatenate([_dot(k_out1, v_all), _dot(k_out2, v_all)], axis=0)
    yield
    alast = alast_tab[...]
    for s0 in range(0, 2 * V7X_LANES, V7X_LANES):
        for c0 in range(0, RET_HEADS * RET_DV, V7X_LANES):
            ret_state[s0:s0 + V7X_LANES, c0:c0 + V7X_LANES] = (
                ret_state[s0:s0 + V7X_LANES, c0:c0 + V7X_LANES] * alast
                + upd[s0:s0 + V7X_LANES, c0:c0 + V7X_LANES])
    intras = [_dot((scores[h] * decay_tab[h]).astype(BF16),
                   proj_ref[0, rows, 512 + 128 * h:640 + 128 * h]) for h in range(RET_HEADS)]
    for h in range(RET_HEADS):
        o = intras[h] + inters[h] * qdec_tab[h]
        mu = jnp.mean(o, axis=-1, keepdims=True)
        d = o - mu
        var = jnp.mean(d * d, axis=-1, keepdims=True)
        on = d * lax.rsqrt(var + EPS) * gnw_ref[:, 128 * h:128 * h + 128]
        g = proj_ref[0, rows, 1024 + 128 * h:1152 + 128 * h].astype(F32)
        store_ret(rows, slice(128 * h, 128 * h + 128), (on * (g * _sigmoid(g))).astype(BF16))


def _ssd_stages(r0, proj_ref, gate_ref, gate_t_ref, convw_ref, convb_ref, dtb_ref, dtb_t_ref,
                alog_ref, alog_t_ref, dskip_ref, normw_ref, store_ssd,
                ssd_state, conv_buf, xbc_buf):
    c = CHUNK
    rows = slice(r0, r0 + c)
    row_i = lax.broadcasted_iota(jnp.int32, (c, c), 0)
    col_j = lax.broadcasted_iota(jnp.int32, (c, c), 1)
    causal = row_i >= col_j
    lane = lax.broadcasted_iota(jnp.int32, (1, V7X_LANES), 1)

    conv_buf[8:8 + c, :] = proj_ref[0, rows, 2048:2816].astype(F32)
    acc = jnp.zeros((c, SSD_CONV_DIM), F32) + convb_ref[...]
    for k in range(SSD_CONV):
        off = 8 - (SSD_CONV - 1) + k
        acc = acc + convw_ref[k:k + 1, :] * conv_buf[off:off + c, :]
    conv_buf[0:8, :] = conv_buf[c:c + 8, :]
    xbc_buf[...] = acc * _sigmoid(acc)
    xs = xbc_buf[:, 0:SSD_INNER]
    bm = xbc_buf[:, 512:640]
    cm = xbc_buf[:, 640:768]

    dt = _softplus(gate_ref[0, rows, :] + dtb_ref[...])
    la = -dt * jnp.exp(alog_ref[...])
    dt_t = _softplus(gate_t_ref[0, :, rows] + dtb_t_ref[...])
    la_t = -dt_t * jnp.exp(alog_t_ref[...])
    yield
    tri = jnp.where(causal, 1.0, 0.0).astype(BF16)
    tri_t = jnp.where(row_i <= col_j, 1.0, 0.0).astype(BF16)
    cum = _tri_dot(tri, la)
    cum_t = _dot_tri(la_t, tri_t)
    lane_half = lane // SSD_HEAD_DIM
    heads_per_group = SSD_HEADS // SSD_GROUPS
    cgs, bgs, gmats, xms = [], [], [], []
    for g in range(SSD_GROUPS):
        gm = lane_half == g
        cgs.append(jnp.where(gm, cm, 0.0))
        bgs.append(jnp.where(gm, bm, 0.0))
        gmats.append(_dot_nt(cgs[g].astype(BF16), bgs[g].astype(BF16)))
    for h in range(SSD_HEADS):
        pair, half = h // 2, h % 2
        xms.append(jnp.where(lane_half == half, xs[:, 128 * pair:128 * pair + 128], 0.0).astype(BF16))
    yield
    cum_last = cum[c - 1:c, :]
    q_scale = jnp.exp(cum)
    k_scale_ssd = dt * jnp.exp(cum_last - cum)
    s_scale = jnp.exp(cum_last)
    inters, updates, probs = [], [], []
    for h in range(SSD_HEADS):
        g = h // heads_per_group
        s_h = ssd_state[h]
        inters.append(_dot((cgs[g] * q_scale[:, h:h + 1]).astype(BF16), s_h.astype(BF16)))
        updates.append(s_h * s_scale[:, h:h + 1] + _dot_tn(bgs[g] * k_scale_ssd[:, h:h + 1], xms[h]))
        cc = cum[:, h:h + 1]
        cr = cum_t[h:h + 1, :]
        lmat = jnp.exp(jnp.where(causal, cc - cr, -jnp.inf)) * dt_t[h:h + 1, :]
        probs.append((gmats[g] * lmat).astype(BF16))
    yield
    intras = [_dot(probs[h], xms[h]) for h in range(SSD_HEADS)]
    for h in range(SSD_HEADS):
        ssd_state[h] = updates[h]
    y_pairs =[intras[2 * pr] + inters[2 * pr] + intras[2 * pr + 1] + inters[2 * pr + 1]
               for pr in range(SSD_HEADS // 2)]
    y = jnp.concatenate(y_pairs, axis=-1) + dskip_ref[...] * xbc_buf[:, 0:SSD_INNER]
    z = proj_ref[0, rows, 1536:2048].astype(F32)
    y = y * (z * _sigmoid(z))
    group = SSD_INNER // SSD_GROUPS
    for g in range(SSD_GROUPS):
        store_ssd(rows, slice(group * g, group * (g + 1)),
                  _rms(y[:, group * g:group * (g + 1)],
                       normw_ref[:, group * g:group * (g + 1)]).astype(BF16))


def _tile_maps(n_tiles, tiles_per_seq):
    def mixer3(i):
        tile = jnp.minimum(i, n_tiles - 1)
        return tile // tiles_per_seq, tile % tiles_per_seq, 0

    def mixer_t(i):
        tile = jnp.minimum(i, n_tiles - 1)
        return tile // tiles_per_seq, 0, tile % tiles_per_seq

    def mixer_pos(i):
        return jnp.minimum(i, n_tiles - 1) % tiles_per_seq, 0

    def ffn(i):
        return jnp.maximum(i - 1, 0), 0

    return mixer3, mixer_t, mixer_pos, ffn


def _ffn_weight_specs(half, j, layer):
    return [
        _layer_spec((half, D_MODEL), j, 0),
        _layer_spec((half, D_MODEL), j, 1),
        _const_spec((1, D_MODEL)),
        _layer_spec((D_MODEL, FFN_HIDDEN), layer),
        _layer_spec((D_MODEL, FFN_HIDDEN), layer),
        _layer_spec((FFN_HIDDEN, D_MODEL), layer),
    ]


def _even_ffn_kernel(proj_ref, gate_ref, gate_t_ref, cos_ref, sin_ref, gnw_ref,
                     convw_ref, convb_ref, dtb_ref, dtb_t_ref, alog_ref, alog_t_ref,
                     dskip_ref, normw_ref,
                     h_ref, woa_ref, wob_ref, nw_ref, wg_ref, wu_ref, wd_ref,
                     out_ref,
                     ret_state, ssd_state, conv_buf, xbc_buf, decay_tab, qdec_tab, kdec_tab,
                     alast_tab, yret_scr, yssd_scr, act_ref, *, n_tiles, tiles_per_seq):
    i = pl.program_id(0)
    tile = jnp.minimum(i, n_tiles - 1)
    slot = i % 2

    @pl.when(i == 0)
    def _():
        _even_tables(decay_tab, qdec_tab, kdec_tab, alast_tab)
        yret_scr[...] = jnp.zeros_like(yret_scr)
        yssd_scr[...] = jnp.zeros_like(yssd_scr)

    @pl.when(tile % tiles_per_seq == 0)
    def _():
        ret_state[...] = jnp.zeros_like(ret_state)
        ssd_state[...] = jnp.zeros_like(ssd_state)
        conv_buf[0:8, :] = jnp.zeros((8, SSD_CONV_DIM), F32)

    ya = yret_scr[1 - slot]
    yb = yssd_scr[1 - slot]

    def store_ret(rows, lanes, value):
        yret_scr[slot, rows, lanes] = value

    def store_ssd(rows, lanes, value):
        yssd_scr[slot, rows, lanes] = value

    def stages():
        for ci in range(ROW_TILE // CHUNK):
            yield from _zip_stages(
                _ssd_stages(ci * CHUNK, proj_ref, gate_ref, gate_t_ref, convw_ref, convb_ref,
                            dtb_ref, dtb_t_ref, alog_ref, alog_t_ref, dskip_ref, normw_ref,
                            store_ssd, ssd_state, conv_buf, xbc_buf),
                _retention_stages(ci * CHUNK, proj_ref, cos_ref, sin_ref, gnw_ref, store_ret,
                                  ret_state, decay_tab, qdec_tab, kdec_tab, alast_tab))
            yield

    _ffn_tile(ya, yb, h_ref[...], woa_ref, wob_ref, nw_ref, wg_ref, wu_ref, wd_ref,
              act_ref, out_ref, stages())


def _even_mixer_ffn(proj, gates, gates_t, cos, sin, gn_w, conv_w, conv_b, dt_b, dt_b_t,
                    a_log, a_log_t, d_skip, norm_w, h2d, w_out, j, ffn_norm_w, w_gate, w_up, w_down,
                    layer):
    b, t_len, _ = proj.shape
    c = CHUNK
    half = RET_HEADS * RET_DV
    tiles_per_seq = t_len // ROW_TILE
    n_tiles = b * tiles_per_seq
    mixer3, mixer_t, mixer_pos, ffn = _tile_maps(n_tiles, tiles_per_seq)
    kern = functools.partial(_even_ffn_kernel, n_tiles=n_tiles, tiles_per_seq=tiles_per_seq)
    return pl.pallas_call(
        kern,
        grid=(n_tiles + 1,),
        in_specs=[
            pl.BlockSpec((1, ROW_TILE, AB_MAIN), mixer3),
            pl.BlockSpec((1, ROW_TILE, GATE_PAD), mixer3),
            pl.BlockSpec((1, 8, ROW_TILE), mixer_t),
            pl.BlockSpec((ROW_TILE, V7X_LANES), mixer_pos),
            pl.BlockSpec((ROW_TILE, V7X_LANES), mixer_pos),
            _const_spec(gn_w.shape), _const_spec(conv_w.shape), _const_spec(conv_b.shape),
            _const_spec(dt_b.shape), _const_spec(dt_b_t.shape),
            _const_spec(a_log.shape), _const_spec(a_log_t.shape),
            _const_spec(d_skip.shape), _const_spec(norm_w.shape),
            pl.BlockSpec((ROW_TILE, D_MODEL), ffn),
        ] + _ffn_weight_specs(half, j, layer),
        out_specs=pl.BlockSpec((ROW_TILE, D_MODEL), ffn),
        out_shape=jax.ShapeDtypeStruct(h2d.shape, F32),
        scratch_shapes=[
            pltpu.VMEM((2 * V7X_LANES, RET_HEADS * RET_DV), F32),
            pltpu.VMEM((SSD_HEADS, V7X_LANES, V7X_LANES), F32),
            pltpu.VMEM((c + 8, SSD_CONV_DIM), F32),
            pltpu.VMEM((c, SSD_CONV_DIM), F32),
            pltpu.VMEM((RET_HEADS, c, c), F32),
            pltpu.VMEM((RET_HEADS, c, V7X_LANES), F32),
            pltpu.VMEM((c, V7X_LANES), F32),
            pltpu.VMEM((V7X_LANES, V7X_LANES), F32),
            pltpu.VMEM((2, ROW_TILE, half), BF16),
            pltpu.VMEM((2, ROW_TILE, SSD_INNER), BF16),
            pltpu.VMEM((ROW_TILE, FFN_HIDDEN), BF16),
        ],
        compiler_params=_params(("arbitrary",)),
        name="even_mixer_ffn",
    )(proj, gates, gates_t, cos, sin, gn_w, conv_w, conv_b, dt_b, dt_b_t,
      a_log, a_log_t, d_skip, norm_w, h2d, w_out, w_out, ffn_norm_w, w_gate, w_up, w_down)


def _hgrn2_block(proj_ref, lbl_ref, normw_ref, store_y, state_t, layer):
    c = CHUNK
    n_chunks = ROW_TILE // c

    logits = lbl_ref[...]
    e = jnp.exp(logits - jnp.max(logits, axis=0, keepdims=True))
    prob = e / jnp.sum(e, axis=0, keepdims=True)
    lb = jnp.zeros((1, HG_HEADS * HG_DIM), F32)
    for i in range(1, layer + 1):
        lb = lb + prob[i:i + 1, :]

    row_i = lax.broadcasted_iota(jnp.int32, (c, c), 0)
    col_j = lax.broadcasted_iota(jnp.int32, (c, c), 1)
    tri = jnp.where(row_i >= col_j, 1.0, 0.0).astype(BF16)

    width = HG_HEADS * HG_DIM

    carried = {"states": [None] * HG_HEADS}

    def gates(ci):
        rows = slice(ci * c, (ci + 1) * c)
        zf = proj_ref[0, rows, width:2 * width].astype(F32)
        f = lb + (1.0 - lb) * _sigmoid(zf)
        carried["k_all"] = (1.0 - lb) * _sigmoid(-zf)
        carried["cum_all"] = _tri_dot(tri, jnp.log(f))

    def head_scores(ci, h):
        rows = slice(ci * c, (ci + 1) * c)
        k_all, cum_all = carried["k_all"], carried["cum_all"]
        sl = slice(HG_DIM * h, HG_DIM * (h + 1))
        q = proj_ref[0, rows, HG_DIM * h:HG_DIM * (h + 1)].astype(F32)
        k = k_all[:, sl]
        cum = cum_all[:, sl]
        v = proj_ref[0, rows, 2 * width + HG_DIM * h:2 * width + HG_DIM * (h + 1)]
        st = state_t[h] if ci == 0 else carried["states"][h]
        inter = _dot_nt((q * jnp.exp(cum)).astype(BF16), st.astype(BF16))
        n_blk = c // SUB
        ends = [cum[(j + 1) * SUB - 1:(j + 1) * SUB, :] for j in range(n_blk)]
        starts = [jnp.zeros((1, HG_DIM), F32)] + ends[:-1]
        start_rows = jnp.concatenate(
            [jnp.broadcast_to(s, (SUB, HG_DIM)) for s in starts], axis=0)
        end_rows = jnp.concatenate(
            [jnp.broadcast_to(e, (SUB, HG_DIM)) for e in ends], axis=0)
        q_blk = (q * jnp.exp(cum - start_rows)).astype(BF16)
        k_end = k * jnp.exp(end_rows - cum)
        k_end_b = k_end.astype(BF16)
        k_own = (k * jnp.exp(start_rows - cum)).astype(BF16)
        blocks = []
        for i in range(n_blk):
            pieces = []
            for j in range(i):
                if j == i - 1:
                    pieces.append(k_end_b[j * SUB:(j + 1) * SUB, :])
                else:
                    gap = jnp.exp(starts[i] - ends[j])
                    pieces.append((k_end[j * SUB:(j + 1) * SUB, :] * gap).astype(BF16))
            pieces.append(k_own[i * SUB:(i + 1) * SUB, :])
            if i + 1 < n_blk:
                pieces.append(jnp.zeros((c - (i + 1) * SUB, HG_DIM), BF16))
            kb = jnp.concatenate(pieces, axis=0)
            blocks.append(_dot_nt(q_blk[i * SUB:(i + 1) * SUB, :], kb))
        scores = jnp.where(row_i >= col_j, jnp.concatenate(blocks, axis=0), 0.0).astype(BF16)
        cl = cum[c - 1:c, :]
        k_out = (k * jnp.exp(cl - cum)).astype(BF16)
        return rows, sl, h, inter, scores, v, st * jnp.exp(cl), k_out

    def head_output(ci, rows, sl, h, inter, scores, v, st_decayed, k_out):
        o = inter + _dot(scores, v)
        new_state = st_decayed + _dot_tn(v.astype(F32), k_out)
        if ci == n_chunks - 1:
            state_t[h] = new_state
        else:
            carried["states"][h] = new_state
        g = proj_ref[0, rows, 3 * width + HG_DIM * h:3 * width + HG_DIM * (h + 1)].astype(F32)
        store_y(rows, sl, (_rms(o, normw_ref[...]) * (g * _sigmoid(g))).astype(BF16))

    for ci in range(n_chunks):
        gates(ci)
        yield
        partial = [head_scores(ci, h) for h in range(HG_HEADS)]
        yield
        for parts in partial:
            head_output(ci, *parts)
        yield


def _hgrn2_ffn_kernel(proj_ref, lbl_ref, hnw_ref, yfox_ref, h_ref,
                      woa_ref, wob_ref, nw_ref, wg_ref, wu_ref, wd_ref,
                      out_ref, state_t, y_scr, act_ref, *, layer, n_tiles, tiles_per_seq):
    i = pl.program_id(0)
    tile = jnp.minimum(i, n_tiles - 1)
    slot = i % 2

    @pl.when(i == 0)
    def _():
        y_scr[...] = jnp.zeros_like(y_scr)

    @pl.when(tile % tiles_per_seq == 0)
    def _():
        state_t[...] = jnp.zeros_like(state_t)

    ya = y_scr[1 - slot]

    def store_y(rows, lanes, value):
        y_scr[slot, rows, lanes] = value

    stages = _hgrn2_block(proj_ref, lbl_ref, hnw_ref, store_y, state_t, layer)
    _ffn_tile(ya, yfox_ref[...], h_ref[...], woa_ref, wob_ref, nw_ref, wg_ref, wu_ref, wd_ref,
              act_ref, out_ref, stages)


def _hgrn2_ffn(proj, lb_logits, hg_norm_w, y_fox, h2d, w_out, j, ffn_norm_w, w_gate, w_up, w_down,
               layer):
    b, t_len, _ = proj.shape
    width = HG_HEADS * HG_DIM
    tiles_per_seq = t_len // ROW_TILE
    n_tiles = b * tiles_per_seq
    mixer3, _, _, ffn = _tile_maps(n_tiles, tiles_per_seq)
    kern = functools.partial(_hgrn2_ffn_kernel, layer=j, n_tiles=n_tiles,
                             tiles_per_seq=tiles_per_seq)
    return pl.pallas_call(
        kern,
        grid=(n_tiles + 1,),
        in_specs=[
            pl.BlockSpec((1, ROW_TILE, 4 * width), mixer3),
            _const_spec(lb_logits.shape),
            _const_spec(hg_norm_w.shape),
            pl.BlockSpec((ROW_TILE, width), ffn),
            pl.BlockSpec((ROW_TILE, D_MODEL), ffn),
        ] + _ffn_weight_specs(width, j, layer),
        out_specs=pl.BlockSpec((ROW_TILE, D_MODEL), ffn),
        out_shape=jax.ShapeDtypeStruct(h2d.shape, F32),
        scratch_shapes=[
            pltpu.VMEM((HG_HEADS, HG_DIM, HG_DIM), F32),
            pltpu.VMEM((2, ROW_TILE, width), BF16),
            pltpu.VMEM((ROW_TILE, FFN_HIDDEN), BF16),
        ],
        compiler_params=_params(("arbitrary",)),
        name="hgrn2_ffn",
    )(proj, lb_logits, hg_norm_w, y_fox, h2d, w_out, w_out, ffn_norm_w, w_gate, w_up, w_down)


def _forget_bias_kernel(gate_ref, bias_ref, out_ref):
    t_len = gate_ref.shape[1]
    blk = V7X_LANES
    row_i = lax.broadcasted_iota(jnp.int32, (blk, blk), 0)
    col_j = lax.broadcasted_iota(jnp.int32, (blk, blk), 1)
    tri = jnp.where(row_i >= col_j, 1.0, 0.0).astype(BF16)
    lane = lax.broadcasted_iota(jnp.int32, (1, V7X_LANES), 1)
    offset = jnp.zeros((1, V7X_LANES), F32)
    for i in range(t_len // blk):
        x = gate_ref[0, i * blk:(i + 1) * blk, :] + bias_ref[...]
        log_f = jnp.minimum(x, 0.0) - jnp.log1p(jnp.exp(-jnp.abs(x)))
        local = _tri_dot(tri, log_f)
        hi, mid, lo = _split3((local + offset) * -LOG2_E)
        offset = offset + local[blk - 1:blk, :]
        hi, mid, lo = hi.astype(F32), mid.astype(F32), lo.astype(F32)
        for h in range(FOX_HEADS):
            parts = jnp.where(lane == 0, hi[:, h:h + 1],
                              jnp.where(lane == 1, mid[:, h:h + 1],
                                        jnp.where(lane == 2, lo[:, h:h + 1], 0.0)))
            out_ref[0, i * blk:(i + 1) * blk, h * V7X_LANES:(h + 1) * V7X_LANES] = parts.astype(BF16)


def _forget_bias(gates, bias):
    b, t_len, _ = gates.shape
    return pl.pallas_call(
        _forget_bias_kernel,
        grid=(b,),
        in_specs=[pl.BlockSpec((1, t_len, GATE_PAD), lambda i: (i, 0, 0)), _const_spec(bias.shape)],
        out_specs=pl.BlockSpec((1, t_len, FOX_HEADS * V7X_LANES), lambda i: (i, 0, 0)),
        out_shape=jax.ShapeDtypeStruct((b, t_len, FOX_HEADS * V7X_LANES), BF16),
        compiler_params=_params(("parallel",)),
        name="forget_bias",
    )(gates, bias)


def _fox_kernel(q_ref, k_ref, kb_ref, v_ref, y_ref):
    sub, keys = ATT_SUB, ATT_KEYS
    n_sub = ATT_ROWS // sub
    qi = pl.program_id(2)
    q0 = pl.multiple_of(qi * ATT_ROWS, ATT_ROWS)
    lane = lax.broadcasted_iota(jnp.int32, (sub, V7X_LANES), 1)
    q_ones = jnp.where(lane < 3, 1.0, 0.0).astype(BF16)

    def scores(r, k):
        q = jnp.concatenate([q_ref[0, r * sub:(r + 1) * sub, :], q_ones], axis=-1)
        return _dot_nt(q, k)

    def update(state, s, v, mask):
        m, acc = state
        if mask is not None:
            s = jnp.where(mask, s, -jnp.inf)
        m_new = jnp.maximum(m, jnp.max(s, axis=-1, keepdims=True))
        alpha = jnp.exp2(m - m_new)
        p = jnp.exp2(s - m_new)
        acc = alpha * acc + _dot(p.astype(BF16), v)
        return m_new, acc

    def tiles(states, pairs, k, v):
        ss = [scores(r, k) for r, _ in pairs]
        for (r, mask), s in zip(pairs, ss):
            states[r] = update(states[r], s, v, mask)
        return states

    def load(k0, width):
        k = jnp.concatenate([k_ref[0, pl.ds(k0, width), :], kb_ref[0, pl.ds(k0, width), :]], axis=-1)
        v = jnp.concatenate([v_ref[0, pl.ds(k0, width), :], jnp.ones((width, V7X_LANES), BF16)],
                            axis=-1)
        return k, v

    def before(kj, states):
        k, v = load(pl.multiple_of(kj * keys, keys), keys)
        return tuple(tiles(list(states), [(r, None) for r in range(n_sub)], k, v))

    init = tuple((jnp.full((sub, 1), -jnp.inf, F32), jnp.zeros((sub, 2 * FOX_DIM), F32))
                 for _ in range(n_sub))
    states = list(lax.fori_loop(0, qi * (ATT_ROWS // keys), before, init))

    row_i = lax.broadcasted_iota(jnp.int32, (sub, sub), 0)
    col_j = lax.broadcasted_iota(jnp.int32, (sub, sub), 1)
    causal = row_i >= col_j
    blocks = [load(pl.multiple_of(q0 + c * sub, sub), sub) for c in range(n_sub)]
    diag_scores = [[scores(r, blocks[c][0]) for r in range(c, n_sub)] for c in range(n_sub)]
    for c in range(n_sub):
        for r in range(c, n_sub):
            states[r] = update(states[r], diag_scores[c][r - c], blocks[c][1],
                               causal if r == c else None)
    for r in range(n_sub):
        _, acc = states[r]
        y_ref[0, r * sub:(r + 1) * sub, :] = (acc[:, 0:FOX_DIM] / acc[:, FOX_DIM:]).astype(y_ref.dtype)


def _fox(proj, key_bias):
    b, t_len, _ = proj.shape
    blk = ATT_ROWS
    qcol, kcol, vcol = (2048 // FOX_DIM, 2560 // FOX_DIM, 3072 // FOX_DIM)
    return pl.pallas_call(
        _fox_kernel,
        grid=(b, FOX_HEADS, t_len // blk),
        in_specs=[
            pl.BlockSpec((1, blk, FOX_DIM), lambda i, h, j: (i, j, qcol + h)),
            pl.BlockSpec((1, t_len, FOX_DIM), lambda i, h, j: (i, 0, kcol + h)),
            pl.BlockSpec((1, t_len, V7X_LANES), lambda i, h, j: (i, 0, h)),
            pl.BlockSpec((1, t_len, FOX_DIM), lambda i, h, j: (i, 0, vcol + h)),
        ],
        out_specs=pl.BlockSpec((1, blk, FOX_DIM), lambda i, h, j: (i, j, h)),
        out_shape=jax.ShapeDtypeStruct((b, t_len, FOX_HEADS * FOX_DIM), BF16),
        compiler_params=_params(("parallel", "parallel", "arbitrary")),
        name="fox_attention",
    )(proj, proj, key_bias, proj)


def _rotary_perm():
    width = RET_HEADS * RET_DK
    perm = np.zeros((2 * width,), np.int32)
    for base in (0, width):
        for half in range(2):
            for h in range(RET_HEADS):
                for i in range(RET_DK // 2):
                    perm[base + half * 128 + h * 32 + i] = base + h * RET_DK + 2 * i + half
    return perm


def _pad_lanes(a, width=GATE_PAD):
    return jnp.pad(a, [(0, 0)] * (a.ndim - 1) + [(0, width - a.shape[-1])])


def kernel(x, norm_mix, norm_ffn, ffn_w_gate, ffn_w_up, ffn_w_down, ab_w_in, ab_w_out, ret_gn_w,
           ssd_conv_w, ssd_conv_b, ssd_dt_bias, ssd_a_log, ssd_d, ssd_norm_w, cd_w_in, cd_w_out,
           hg_lb_logits, hg_norm_w, fox_f_bias, fox_q_norm_w, fox_k_norm_w):
    b, t_len, d = x.shape
    depth = norm_mix.shape[0]
    n_rows = b * t_len
    perm = _rotary_perm()

    half = RET_DK // 2
    freqs = ROPE_BASE ** (-jnp.linspace(0.0, 1.0, half, dtype=F32))
    ang = jnp.arange(t_len, dtype=F32)[:, None] * freqs[None, :]
    cos = jnp.tile(jnp.cos(ang), (1, RET_HEADS))
    sin = jnp.tile(jnp.sin(ang), (1, RET_HEADS))

    ab_w_out_b, cd_w_out_b = ab_w_out.astype(BF16), cd_w_out.astype(BF16)
    w_gate_b, w_up_b, w_down_b = (ffn_w_gate.astype(BF16), ffn_w_up.astype(BF16),
                                  ffn_w_down.astype(BF16))

    h2d = x.reshape(n_rows, d)
    for layer in range(depth):
        j = layer // 2
        if layer % 2 == 0:
            w_qk = ab_w_in[j, :, 0:2 * RET_HEADS * RET_DK][:, perm]
            proj, gates = _inproj(h2d, norm_mix[layer][None, :], ab_w_in, j, AB_MAIN, w_qk=w_qk)
            gates3 = gates.reshape(b, t_len, GATE_PAD)
            gates_t = jnp.transpose(gates3[:, :, 0:8], (0, 2, 1))
            h2d = _even_mixer_ffn(
                proj.reshape(b, t_len, AB_MAIN), gates3, gates_t, cos, sin,
                ret_gn_w[j].reshape(1, RET_HEADS * RET_DV),
                ssd_conv_w[j], ssd_conv_b[j][None, :],
                _pad_lanes(ssd_dt_bias[j][None, :]), ssd_dt_bias[j][:, None],
                _pad_lanes(ssd_a_log[j][None, :]), ssd_a_log[j][:, None],
                jnp.repeat(ssd_d[j], SSD_HEAD_DIM)[None, :], ssd_norm_w[j][None, :],
                h2d, ab_w_out_b, j, norm_ffn[layer][None, :], w_gate_b, w_up_b, w_down_b, layer)
        else:
            qk_w = jnp.concatenate([jnp.tile(fox_q_norm_w[j] * (FOX_DIM ** -0.5 * LOG2_E), FOX_HEADS),
                                    jnp.tile(fox_k_norm_w[j], FOX_HEADS)])[None, :]
            proj, gates = _inproj(h2d, norm_mix[layer][None, :], cd_w_in, j, CD_MAIN,
                                  qk_w=qk_w, norm_cols=(2048, 3072))
            proj3 = proj.reshape(b, t_len, CD_MAIN)
            key_bias = _forget_bias(gates.reshape(b, t_len, GATE_PAD),
                                    _pad_lanes(fox_f_bias[j][None, :]))
            y_fox = _fox(proj3, key_bias).reshape(n_rows, FOX_HEADS * FOX_DIM)
            h2d = _hgrn2_ffn(proj3, hg_lb_logits, hg_norm_w[j][None, :], y_fox, h2d, cd_w_out_b, j,
                             norm_ffn[layer][None, :], w_gate_b, w_up_b, w_down_b, layer)
    return h2d.reshape(b, t_len, d)
```
